```python
import math
import jax
import jax.numpy as jnp
from jax import lax
import numpy as np

D_MODEL = 2048
BATCH = 1
SEQ = 8192
DEPTH = 2

CHUNK = 64
Q_BLOCK = 128
EPS = 1e-6
SB_HEADS = 4
HEAD_DIM = 128
SB_WIDTH = SB_HEADS * HEAD_DIM
FOX_HEADS = 4
FOX_WIDTH = FOX_HEADS * HEAD_DIM
SSM_HEAD_DIM = 64
SSM_HEADS = 16
SSM_INNER = SSM_HEADS * SSM_HEAD_DIM
SSM_GROUPS = 2
SSM_STATE = 128
CONV_WIDTH = 4
SSM_CONV_DIM = SSM_INNER + 2 * SSM_GROUPS * SSM_STATE
N_BRANCH = 3
IN_PROJ_SIZES = (SB_WIDTH, SB_WIDTH, SB_WIDTH,
                 FOX_WIDTH, FOX_WIDTH, FOX_WIDTH, FOX_HEADS,
                 SSM_INNER, SSM_CONV_DIM, SSM_HEADS,
                 N_BRANCH * D_MODEL)
IN_PROJ_WIDTH = sum(IN_PROJ_SIZES)
N_EXPERTS = 16
N_EXPERT_GROUPS = 4
EXPERTS_PER_GROUP = N_EXPERTS // N_EXPERT_GROUPS
TOPK_GROUPS = 1
TOP_K = 2
D_FF_EXPERT = 1024

kernel_name = 'hybrid_sb_fox_ssd_moe_encoder'


def rms_norm(x, gain):
    xf = x.astype(jnp.float32)
    y = xf * lax.rsqrt(jnp.mean(xf * xf, axis=-1, keepdims=True) + EPS)
    return (y * gain.astype(jnp.float32)).astype(x.dtype)


def split_heads(u, n_heads):
    b, s, _ = u.shape
    return u.reshape(b, s, n_heads, -1)


def query_blocks(u):
    b, s = u.shape[:2]
    u = u.reshape((b, s // Q_BLOCK, Q_BLOCK) + u.shape[2:])
    return jnp.moveaxis(u, 1, 0)


def merge_query_blocks(u):
    u = jnp.moveaxis(u, 0, 1)
    return u.reshape((u.shape[0], -1) + u.shape[3:])


def stick_breaking_attention(q, k, v):
    s_len, d = q.shape[1], q.shape[-1]
    scale = d ** -0.5
    k_pos = jnp.arange(s_len)

    def block(args):
        q_blk, blk = args
        q_pos = blk * Q_BLOCK + jnp.arange(Q_BLOCK)
        z = jnp.einsum('bqhd,bkhd->bhqk', q_blk, k).astype(jnp.float32) * scale
        past = k_pos[None, :] < q_pos[:, None]
        log_keep = jnp.where(past, jax.nn.log_sigmoid(-z), 0.0)
        suffix = lax.cumsum(log_keep, axis=3, reverse=True)
        between = jnp.concatenate([suffix[..., 1:], jnp.zeros_like(suffix[..., :1])], axis=-1)
        weight = jnp.where(past, jnp.exp(jax.nn.log_sigmoid(z) + between), 0.0)
        return jnp.einsum('bhqk,bkhd->bqhd', weight.astype(v.dtype), v)

    n_blk = s_len // Q_BLOCK
    out = lax.map(block, (query_blocks(q), jnp.arange(n_blk)))
    return merge_query_blocks(out)


def forgetting_attention(q, k, v, log_f):
    s_len, d = q.shape[1], q.shape[-1]
    scale = d ** -0.5
    k_pos = jnp.arange(s_len)
    f_cum = lax.cumsum(log_f.astype(jnp.float32), axis=1)
    f_key = jnp.swapaxes(f_cum, 1, 2)[:, :, None, :]

    def block(args):
        q_blk, fq_blk, blk = args
        q_pos = blk * Q_BLOCK + jnp.arange(Q_BLOCK)
        logits = (jnp.einsum('bqhd,bkhd->bhqk', q_blk, k).astype(jnp.float32) * scale
                  + jnp.swapaxes(fq_blk, 1, 2)[..., None] - f_key)
        mask = k_pos[None, :] <= q_pos[:, None]
        p = jax.nn.softmax(jnp.where(mask, logits, -jnp.inf), axis=-1)
        return jnp.einsum('bhqk,bkhd->bqhd', p.astype(v.dtype), v)

    n_blk = s_len // Q_BLOCK
    out = lax.map(block, (query_blocks(q), query_blocks(f_cum), jnp.arange(n_blk)))
    return merge_query_blocks(out)


def causal_depthwise_conv(u, w, bias):
    width, ch = w.shape
    out = lax.conv_general_dilated(u, w[:, None, :], window_strides=(1,),
                                   padding=[(width - 1, 0)],
                                   dimension_numbers=('NWC', 'WIO', 'NWC'),
                                   feature_group_count=ch)
    return out + bias


def ssd_chunked(x, dt, a, b_mat, c_mat):
    bsz, s_len, n_h, p = x.shape
    g, n = b_mat.shape[-2:]
    hg = n_h // g
    nc = s_len // CHUNK
    xc = (x * dt[..., None]).reshape(bsz, nc, CHUNK, g, hg, p)
    log_a = (dt * a).reshape(bsz, nc, CHUNK, g, hg)
    bc = b_mat.reshape(bsz, nc, CHUNK, g, n)
    cc = c_mat.reshape(bsz, nc, CHUNK, g, n)
    a_cum = lax.cumsum(log_a, axis=2)
    idx = jnp.arange(CHUNK)
    causal = (idx[:, None] >= idx[None, :])[:, :, None, None]
    seg = a_cum[:, :, :, None] - a_cum[:, :, None, :]
    decay = jnp.exp(jnp.where(causal, seg, -jnp.inf))
    cb = jnp.einsum('bclgn,bcsgn->bclsg', cc, bc)
    y_diag = jnp.einsum('bclsg,bclsgh,bcsghp->bclghp', cb, decay, xc)
    to_end = jnp.exp(a_cum[:, :, -1:] - a_cum)
    states = jnp.einsum('bclgn,bclgh,bclghp->bcghpn', bc, to_end, xc)
    chunk_decay = jnp.exp(a_cum[:, :, -1])

    def step(h, inp):
        st, dec = inp
        return h * dec[..., None, None] + st, h

    h0 = jnp.zeros_like(states[:, 0])
    _, h_in = lax.scan(step, h0, (jnp.moveaxis(states, 1, 0), jnp.moveaxis(chunk_decay, 1, 0)))
    h_in = jnp.moveaxis(h_in, 0, 1)
    y_off = jnp.einsum('bclgn,bcghpn,bclgh->bclghp', cc, h_in, jnp.exp(a_cum))
    return (y_diag + y_off).reshape(bsz, s_len, n_h, p)


def mamba2_mixer(z, xbc, dt_raw, conv_w, conv_b, dt_bias, a_log, d_skip, g_norm):
    b, s, _ = z.shape
    xbc = jax.nn.silu(causal_depthwise_conv(xbc, conv_w, conv_b))
    xs, b_mat, c_mat = jnp.split(xbc, [SSM_INNER, SSM_INNER + SSM_GROUPS * SSM_STATE], axis=-1)
    x_h = xs.reshape(b, s, SSM_HEADS, SSM_HEAD_DIM)
    dt = jax.nn.softplus((dt_raw + dt_bias).astype(jnp.float32))
    a = -jnp.exp(a_log.astype(jnp.float32))
    y = ssd_chunked(x_h, dt, a,
                    b_mat.reshape(b, s, SSM_GROUPS, SSM_STATE),
                    c_mat.reshape(b, s, SSM_GROUPS, SSM_STATE))
    y = y + d_skip.astype(jnp.float32)[:, None] * x_h
    y = y.reshape(b, s, SSM_INNER).astype(z.dtype)
    return rms_norm(y * jax.nn.silu(z), g_norm)


def grouped_moe(h, w_router, b_router, w_gate, w_up, w_down):
    b, s, d = h.shape
    t = h.reshape(b * s, d)
    n_tok = t.shape[0]
    affinity = jax.nn.sigmoid((t @ w_router).astype(jnp.float32))
    sel = (affinity + b_router.astype(jnp.float32)).reshape(n_tok, N_EXPERT_GROUPS, EXPERTS_PER_GROUP)
    group_score = lax.top_k(sel, 2)[0].sum(-1)
    _, g_idx = lax.top_k(group_score, TOPK_GROUPS)
    group_mask = jax.nn.one_hot(g_idx, N_EXPERT_GROUPS, dtype=jnp.float32).sum(1) > 0
    masked = jnp.where(group_mask[:, :, None], sel, -jnp.inf).reshape(n_tok, N_EXPERTS)
    _, e_idx = lax.top_k(masked, TOP_K)
    w_sel = jnp.take_along_axis(affinity, e_idx, axis=-1)
    w_sel = w_sel / jnp.sum(w_sel, axis=-1, keepdims=True)
    combine = jnp.sum(jax.nn.one_hot(e_idx, N_EXPERTS, dtype=jnp.float32) * w_sel[..., None], axis=1)
    combine = combine.astype(t.dtype)
    y = jnp.zeros_like(t)
    for e in range(N_EXPERTS):
        act = jax.nn.silu(t @ w_gate[e]) * (t @ w_up[e])
        y = y + combine[:, e:e + 1] * (act @ w_down[e])
    return y.reshape(b, s, d)


def setup_inputs(seed: int = 0) -> dict:
    key = jax.random.key(seed)
    ks = iter(jax.random.split(key, 40))
    L, D, E, F = DEPTH, D_MODEL, N_EXPERTS, D_FF_EXPERT

    def nrm(shape, scale):
        return jax.random.normal(next(ks), shape, jnp.float32) * scale

    x = nrm((BATCH, SEQ, D), 1.0)
    c = nrm((BATCH, D), 1.0)
    w_ada = nrm((L, D, 6 * D), 0.5 * D ** -0.5)
    b_ada = nrm((L, 6 * D), 0.02)
    g_norm_mix = 1.0 + nrm((L, D), 0.02)
    w_in = nrm((L, D, IN_PROJ_WIDTH), D ** -0.5)
    b_fgate = 3.0 + nrm((L, FOX_HEADS), 0.5)
    g_q_fox = 1.0 + nrm((L, HEAD_DIM), 0.02)
    g_k_fox = 1.0 + nrm((L, HEAD_DIM), 0.02)
    conv_w = nrm((L, CONV_WIDTH, SSM_CONV_DIM), CONV_WIDTH ** -0.5)
    conv_b = nrm((L, SSM_CONV_DIM), 0.02)
    dt0 = jnp.exp(jax.random.uniform(next(ks), (L, SSM_HEADS), jnp.float32,
                                     math.log(1e-3), math.log(1e-1)))
    dt_bias = dt0 + jnp.log(-jnp.expm1(-dt0))
    a_log = jnp.log(jax.random.uniform(next(ks), (L, SSM_HEADS), jnp.float32, 1.0, 16.0))
    d_skip = 1.0 + nrm((L, SSM_HEADS), 0.02)
    g_ssm_norm = 1.0 + nrm((L, SSM_INNER), 0.02)
    w_branch_sb = nrm((L, SB_WIDTH, D), SB_WIDTH ** -0.5)
    w_branch_fox = nrm((L, FOX_WIDTH, D), FOX_WIDTH ** -0.5)
    w_branch_ssm = nrm((L, SSM_INNER, D), SSM_INNER ** -0.5)
    w_out = nrm((L, D, D), D ** -0.5)
    g_norm_ffn = 1.0 + nrm((L, D), 0.02)
    w_router = nrm((D, E), D ** -0.5)
    b_router = nrm((E,), 0.01)
    w_e_gate = nrm((L, E, D, F), D ** -0.5)
    w_e_up = nrm((L, E, D, F), D ** -0.5)
    w_e_down = nrm((L, E, F, D), F ** -0.5)
    return {'x': x, 'c': c, 'w_ada': w_ada, 'b_ada': b_ada, 'g_norm_mix': g_norm_mix,
            'w_in': w_in, 'b_fgate': b_fgate, 'g_q_fox': g_q_fox, 'g_k_fox': g_k_fox,
            'conv_w': conv_w, 'conv_b': conv_b, 'dt_bias': dt_bias, 'a_log': a_log,
            'd_skip': d_skip, 'g_ssm_norm': g_ssm_norm, 'w_branch_sb': w_branch_sb,
            'w_branch_fox': w_branch_fox, 'w_branch_ssm': w_branch_ssm, 'w_out': w_out,
            'g_norm_ffn': g_norm_ffn, 'w_router': w_router, 'b_router': b_router,
            'w_e_gate': w_e_gate, 'w_e_up': w_e_up, 'w_e_down': w_e_down}


def reference(x, c, w_ada, b_ada, g_norm_mix, w_in, b_fgate, g_q_fox, g_k_fox,
              conv_w, conv_b, dt_bias, a_log, d_skip, g_ssm_norm, w_branch_sb,
              w_branch_fox, w_branch_ssm, w_out, g_norm_ffn, w_router, b_router,
              w_e_gate, w_e_up, w_e_down):
    b, s, d = x.shape
    offsets = []
    acc = 0
    for size in IN_PROJ_SIZES[:-1]:
        acc += size
        offsets.append(acc)
    cond = jax.nn.silu(c)
    for layer in range(DEPTH):
        mod = (cond @ w_ada[layer] + b_ada[layer])[:, None, :]
        shift_m, scale_m, gate_m, shift_f, scale_f, gate_f = jnp.split(mod, 6, axis=-1)

        h = rms_norm(x, g_norm_mix[layer]) * (1.0 + scale_m) + shift_m
        (sb_q, sb_k, sb_v, fx_q, fx_k, fx_v, fx_f,
         ssm_z, ssm_xbc, ssm_dt, gate_logits) = jnp.split(h @ w_in[layer], offsets, axis=-1)

        o_sb = stick_breaking_attention(split_heads(sb_q, SB_HEADS), split_heads(sb_k, SB_HEADS),
                                        split_heads(sb_v, SB_HEADS)).reshape(b, s, SB_WIDTH)
        log_f = jax.nn.log_sigmoid((fx_f + b_fgate[layer]).astype(jnp.float32))
        o_fox = forgetting_attention(rms_norm(split_heads(fx_q, FOX_HEADS), g_q_fox[layer]),
                                     rms_norm(split_heads(fx_k, FOX_HEADS), g_k_fox[layer]),
                                     split_heads(fx_v, FOX_HEADS), log_f).reshape(b, s, FOX_WIDTH)
        o_ssm = mamba2_mixer(ssm_z, ssm_xbc, ssm_dt, conv_w[layer], conv_b[layer],
                             dt_bias[layer], a_log[layer], d_skip[layer], g_ssm_norm[layer])

        gates = jax.nn.sigmoid(gate_logits).reshape(b, s, N_BRANCH, d)
        merged = (gates[:, :, 0] * (o_sb @ w_branch_sb[layer])
                  + gates[:, :, 1] * (o_fox @ w_branch_fox[layer])
                  + gates[:, :, 2] * (o_ssm @ w_branch_ssm[layer]))
        x = x + gate_m * (merged @ w_out[layer])

        h = rms_norm(x, g_norm_ffn[layer]) * (1.0 + scale_f) + shift_f
        x = x + gate_f * grouped_moe(h, w_router, b_router, w_e_gate[layer],
                                     w_e_up[layer], w_e_down[layer])
    return x
```

```python
import functools

import jax
import jax.numpy as jnp
from jax import lax
from jax.experimental import pallas as pl
from jax.experimental.pallas import tpu as pltpu

F32 = jnp.float32
BF16 = jnp.bfloat16
HIGHEST = lax.Precision.HIGHEST
LOG2E = 1.4426950408889634

D_MODEL = 2048
EPS = 1e-6
HEAD_DIM = 128
SB_HEADS = 4
FOX_HEADS = 4
SB_WIDTH = SB_HEADS * HEAD_DIM
FOX_WIDTH = FOX_HEADS * HEAD_DIM
SSM_HEAD_DIM = 64
SSM_HEADS = 16
SSM_INNER = SSM_HEADS * SSM_HEAD_DIM
SSM_GROUPS = 2
SSM_STATE = 128
CONV_WIDTH = 4
SSM_CONV_DIM = SSM_INNER + 2 * SSM_GROUPS * SSM_STATE
N_BRANCH = 3
N_EXPERTS = 16
N_EXPERT_GROUPS = 4
EXPERTS_PER_GROUP = N_EXPERTS // N_EXPERT_GROUPS
D_FF_EXPERT = 1024

LANES = 128
V7X_VMEM_BYTES = 64 * 1024 * 1024
VMEM_LIMIT = 48 * 1024 * 1024

ROW_TILE = 512
ATTN_TILE = 256
ATTN_SUB = 4
SSD_CHUNK = 128
EXPERT_TILE = 256
COMBINE_TILE = 256


def _params(sem, vmem=VMEM_LIMIT):
    return pltpu.CompilerParams(dimension_semantics=sem, vmem_limit_bytes=vmem)


def _sigmoid(x):
    return 1.0 / (1.0 + jnp.exp(-x))


def _log_sigmoid(x):
    return jnp.minimum(x, 0.0) - jnp.log(1.0 + jnp.exp(-jnp.abs(x)))


def _softplus(x):
    return jnp.maximum(x, 0.0) + jnp.log(1.0 + jnp.exp(-jnp.abs(x)))


def _ada_body(c_ref, w_ref, b_ref, o_ref):
    c = c_ref[...]
    cond = c * _sigmoid(c)
    parts = [jnp.sum(w_ref[0, :, j * LANES:(j + 1) * LANES] * cond, axis=0, keepdims=True)
             for j in range(w_ref.shape[2] // LANES)]
    o_ref[0] = jnp.concatenate(parts, axis=1) + b_ref[0]


def _ada(c, w_ada, b_ada):
    n_layers, d, n = w_ada.shape
    tn = 1024
    c_lanes = jnp.broadcast_to(c.reshape(d, 1), (d, LANES))
    out = pl.pallas_call(
        _ada_body,
        grid=(n_layers, n // tn),
        in_specs=[pl.BlockSpec((d, LANES), lambda l, j: (0, 0)),
                  pl.BlockSpec((1, d, tn), lambda l, j: (l, 0, j)),
                  pl.BlockSpec((1, 1, tn), lambda l, j: (l, 0, j))],
        out_specs=pl.BlockSpec((1, 1, tn), lambda l, j: (l, 0, j)),
        out_shape=jax.ShapeDtypeStruct((n_layers, 1, n), F32),
        compiler_params=_params(("arbitrary", "arbitrary")),
        name="ada_mod",
    )(c_lanes, w_ada, b_ada.reshape(n_layers, 1, n))
    return out[:, 0, :]


def _modnorm(x, g, scale, shift):
    y = x * lax.rsqrt(jnp.mean(x * x, axis=-1, keepdims=True) + EPS) * g
    return y * (1.0 + scale) + shift


def _norm_body(x_ref, g_ref, sc_ref, sh_ref, o_ref):
    o_ref[...] = _modnorm(x_ref[...], g_ref[...], sc_ref[...], sh_ref[...]).astype(BF16)


def _norm(x, g, scale, shift):
    t, d = x.shape
    vec = pl.BlockSpec((1, d), lambda i: (0, 0))
    return pl.pallas_call(
        _norm_body,
        grid=(t // ROW_TILE,),
        in_specs=[pl.BlockSpec((ROW_TILE, d), lambda i: (i, 0)), vec, vec, vec],
        out_specs=pl.BlockSpec((ROW_TILE, d), lambda i: (i, 0)),
        out_shape=jax.ShapeDtypeStruct((t, d), BF16),
        compiler_params=_params(("arbitrary",)),
        name="mod_norm",
    )(x, g, scale, shift)


def _mm_body(a_ref, w_ref, o_ref, *, act):
    r = jnp.dot(a_ref[...], w_ref[...], preferred_element_type=F32)
    if act == "sigmoid":
        r = _sigmoid(r)
    o_ref[...] = r.astype(o_ref.dtype)


def _mm(a, w, out_dtype, tn, act=None, tm=1024, name="matmul"):
    m, k = a.shape
    n = w.shape[1]
    tm = min(tm, m)
    return pl.pallas_call(
        functools.partial(_mm_body, act=act),
        grid=(m // tm, n // tn),
        in_specs=[pl.BlockSpec((tm, k), lambda i, j: (i, 0)),
                  pl.BlockSpec((k, tn), lambda i, j: (0, j))],
        out_specs=pl.BlockSpec((tm, tn), lambda i, j: (i, j)),
        out_shape=jax.ShapeDtypeStruct((m, n), out_dtype),
        compiler_params=_params(("arbitrary", "arbitrary")),
        name=name,
    )(a, w)


def _fox_prep_body(q_ref, k_ref, f_ref, bf_ref, gq_ref, gk_ref,
                   qo_ref, ko_ref, fo_ref, carry_ref, *, tm):
    @pl.when(pl.program_id(0) == 0)
    def _():
        carry_ref[...] = jnp.zeros_like(carry_ref)

    scale = HEAD_DIM ** -0.5 * LOG2E
    for h in range(FOX_HEADS):
        sl = slice(h * HEAD_DIM, (h + 1) * HEAD_DIM)
        q = q_ref[:, sl].astype(F32)
        qn = q * lax.rsqrt(jnp.mean(q * q, axis=-1, keepdims=True) + EPS) * gq_ref[...]
        qo_ref[:, sl] = (qn * scale).astype(BF16)
        k = k_ref[:, sl].astype(F32)
        kn = k * lax.rsqrt(jnp.mean(k * k, axis=-1, keepdims=True) + EPS) * gk_ref[...]
        ko_ref[:, sl] = kn.astype(BF16)

    log_f = _log_sigmoid(f_ref[...] + bf_ref[...])
    row = lax.broadcasted_iota(jnp.int32, (tm, tm), 0)
    col = lax.broadcasted_iota(jnp.int32, (tm, tm), 1)
    tri = (col <= row).astype(F32)
    cum = jnp.dot(tri, log_f, precision=HIGHEST, preferred_element_type=F32) + carry_ref[...]
    fo_ref[...] = cum * LOG2E
    carry_ref[...] = cum[tm - 1:tm, :]


def _fox_prep(qkv, misc, b_fgate_pad, g_q, g_k):
    t = qkv.shape[0]
    tm = ROW_TILE
    q_blk = 3 * SB_WIDTH // FOX_WIDTH
    vec = pl.BlockSpec((1, LANES), lambda i: (0, 0))
    return pl.pallas_call(
        functools.partial(_fox_prep_body, tm=tm),
        grid=(t // tm,),
        in_specs=[pl.BlockSpec((tm, FOX_WIDTH), lambda i: (i, q_blk)),
                  pl.BlockSpec((tm, FOX_WIDTH), lambda i: (i, q_blk + 1)),
                  pl.BlockSpec((tm, LANES), lambda i: (i, 0)),
                  vec, vec, vec],
        out_specs=[pl.BlockSpec((tm, FOX_WIDTH), lambda i: (i, 0)),
                   pl.BlockSpec((tm, FOX_WIDTH), lambda i: (i, 0)),
                   pl.BlockSpec((tm, LANES), lambda i: (i, 0))],
        out_shape=[jax.ShapeDtypeStruct((t, FOX_WIDTH), BF16),
                   jax.ShapeDtypeStruct((t, FOX_WIDTH), BF16),
                   jax.ShapeDtypeStruct((t, LANES), F32)],
        scratch_shapes=[pltpu.VMEM((1, LANES), F32)],
        compiler_params=_params(("arbitrary",)),
        name="fox_prep",
    )(qkv, qkv, misc, b_fgate_pad, g_q, g_k)


def _sb_body(q_ref, k_ref, v_ref, o_ref, *, tb, nsub):
    qb = pl.program_id(1)
    qs = [q_ref[a * tb:(a + 1) * tb, :] for a in range(nsub)]
    row = lax.broadcasted_iota(jnp.int32, (tb, tb), 0)
    col = lax.broadcasted_iota(jnp.int32, (tb, tb), 1)
    ones = (row >= col).astype(BF16)
    suffix_ones = jnp.concatenate([ones, ones], axis=0)
    past = col < row

    def load(kb):
        ks = pl.multiple_of(kb * tb, tb)
        return k_ref[pl.ds(ks, tb), :], v_ref[pl.ds(ks, tb), :]

    def sweep(k, v, chains):
        us = [lax.dot_general(q, k, (((1,), (1,)), ((), ())), preferred_element_type=F32)
              for q, _, _, _ in chains]
        splits = []
        for u, (_, _, _, diagonal) in zip(us, chains):
            drop = jnp.maximum(u, 0.0) + jnp.log2(1.0 + jnp.exp2(-jnp.abs(u)))
            if diagonal:
                drop = jnp.where(past, drop, 0.0)
            hi = drop.astype(BF16)
            lo = (drop - hi.astype(F32)).astype(BF16)
            splits.append(jnp.concatenate([hi, lo], axis=1))
        sufs = [jnp.dot(s, suffix_ones, preferred_element_type=F32) for s in splits]
        ws = []
        for u, suf, (_, carry, _, diagonal) in zip(us, sufs, chains):
            w = jnp.exp2(u - suf - carry)
            if diagonal:
                w = jnp.where(past, w, 0.0)
            ws.append(w.astype(BF16))
        return [(carry + suf[:, 0:1], acc + jnp.dot(w, v, preferred_element_type=F32))
                for w, suf, (_, carry, acc, _) in zip(ws, sufs, chains)]

    state = [(jnp.zeros((tb, 1), F32), jnp.zeros((tb, HEAD_DIM), F32)) for _ in range(nsub)]
    for j in reversed(range(nsub)):
        k, v = load(qb * nsub + j)
        new = sweep(k, v, [(qs[a], state[a][0], state[a][1], a == j) for a in range(j, nsub)])
        state[j:] = new

    def step(i, c):
        k, v = load(qb * nsub - 1 - i)
        return tuple(sweep(k, v, [(qs[a], c[a][0], c[a][1], False) for a in range(nsub)]))

    state = lax.fori_loop(0, qb * nsub, step, tuple(state))
    for a in range(nsub):
        o_ref[a * tb:(a + 1) * tb, :] = state[a][1].astype(o_ref.dtype)


def _sb_attention(qkv):
    t = qkv.shape[0]
    tb, nsub = ATTN_TILE, ATTN_SUB
    return pl.pallas_call(
        functools.partial(_sb_body, tb=tb, nsub=nsub),
        grid=(SB_HEADS, t // (tb * nsub)),
        in_specs=[pl.BlockSpec((tb * nsub, HEAD_DIM), lambda h, i: (i, h)),
                  pl.BlockSpec((t, HEAD_DIM), lambda h, i: (0, SB_HEADS + h)),
                  pl.BlockSpec((t, HEAD_DIM), lambda h, i: (0, 2 * SB_HEADS + h))],
        out_specs=pl.BlockSpec((tb * nsub, HEAD_DIM), lambda h, i: (i, h)),
        out_shape=jax.ShapeDtypeStruct((t, SB_WIDTH), BF16),
        compiler_params=_params(("arbitrary", "arbitrary")),
        name="sb_attention",
    )(qkv, qkv, qkv)


def _fox_body(q_ref, k_ref, v_ref, fcol_ref, frow_ref, o_ref, *, tb, nsub):
    h = pl.program_id(0)
    qb = pl.program_id(1)
    lane = lax.broadcasted_iota(jnp.int32, (tb, LANES), 1)
    qs, f_qs = [], []
    for a in range(nsub):
        qs.append(q_ref[a * tb:(a + 1) * tb, :])
        f_qs.append(jnp.sum(jnp.where(lane == h, fcol_ref[a * tb:(a + 1) * tb, :], 0.0),
                            axis=-1, keepdims=True))
    row = lax.broadcasted_iota(jnp.int32, (tb, tb), 0)
    col = lax.broadcasted_iota(jnp.int32, (tb, tb), 1)
    causal = col <= row

    def load(kb):
        ks = pl.multiple_of(kb * tb, tb)
        return (k_ref[pl.ds(ks, tb), :], v_ref[pl.ds(ks, tb), :],
                frow_ref[pl.ds(h, 1), pl.ds(ks, tb)])

    def sweep(k, v, f_k, chains):
        ss = []
        for a, _, diagonal in chains:
            s = lax.dot_general(qs[a], k, (((1,), (1,)), ((), ())), preferred_element_type=F32) - f_k
            ss.append(jnp.where(causal, s, -jnp.inf) if diagonal else s)
        ms, ps, sums = [], [], []
        for s, (a, (m, _, _), _) in zip(ss, chains):
            m_new = jnp.maximum(m, jnp.max(s, axis=-1, keepdims=True) + f_qs[a])
            p = jnp.exp2(s - (m_new - f_qs[a]))
            ms.append(m_new)
            sums.append(jnp.sum(p, axis=-1, keepdims=True))
            ps.append(p.astype(BF16))
        out = []
        for p, p_sum, m_new, (_, (m, l, acc), _) in zip(ps, sums, ms, chains):
            alpha = jnp.exp2(m - m_new)
            out.append((m_new, alpha * l + p_sum,
                        alpha * acc + jnp.dot(p, v, preferred_element_type=F32)))
        return out

    def step(kb, c):
        k, v, f_k = load(kb)
        return tuple(sweep(k, v, f_k, [(a, c[a], False) for a in range(nsub)]))

    init = tuple((jnp.full((tb, 1), -jnp.inf, F32), jnp.zeros((tb, 1), F32),
                  jnp.zeros((tb, HEAD_DIM), F32)) for _ in range(nsub))
    state = list(lax.fori_loop(0, qb * nsub, step, init))
    for j in range(nsub):
        k, v, f_k = load(qb * nsub + j)
        state[j:] = sweep(k, v, f_k, [(a, state[a], a == j) for a in range(j, nsub)])
    for a in range(nsub):
        o_ref[a * tb:(a + 1) * tb, :] = (state[a][2] / state[a][1]).astype(o_ref.dtype)


def _fox_attention(qn, kn, qkv, f_col, f_row):
    t = qn.shape[0]
    tb, nsub = ATTN_TILE, ATTN_SUB
    v_blk = (3 * SB_WIDTH + 2 * FOX_WIDTH) // HEAD_DIM
    return pl.pallas_call(
        functools.partial(_fox_body, tb=tb, nsub=nsub),
        grid=(FOX_HEADS, t // (tb * nsub)),
        in_specs=[pl.BlockSpec((tb * nsub, HEAD_DIM), lambda h, i: (i, h)),
                  pl.BlockSpec((t, HEAD_DIM), lambda h, i: (0, h)),
                  pl.BlockSpec((t, HEAD_DIM), lambda h, i: (0, v_blk + h)),
                  pl.BlockSpec((tb * nsub, LANES), lambda h, i: (i, 0)),
                  pl.BlockSpec((8, t), lambda h, i: (0, 0))],
        out_specs=pl.BlockSpec((tb * nsub, HEAD_DIM), lambda h, i: (i, h)),
        out_shape=jax.ShapeDtypeStruct((t, FOX_WIDTH), BF16),
        compiler_params=_params(("arbitrary", "arbitrary")),
        name="fox_attention",
    )(qn, kn, qkv, f_col, f_row)


def _ssd_body(z_ref, xbc_ref, dt_ref, cw_ref, cb_ref, dtb_ref, a_ref, dsk_ref, gn_ref, ex_ref,
              o_ref, ext_ref, state_ref, *, cl):
    hg = SSM_HEADS // SSM_GROUPS
    gw = hg * SSM_HEAD_DIM

    @pl.when(pl.program_id(0) == 0)
    def _():
        ext_ref[0:8, :] = jnp.zeros((8, SSM_CONV_DIM), F32)
        state_ref[...] = jnp.zeros_like(state_ref)

    ext_ref[8:8 + cl, :] = xbc_ref[...]
    conv = cb_ref[...] + cw_ref[0:1, :] * ext_ref[pl.ds(8 - (CONV_WIDTH - 1), cl), :]
    for w in range(1, CONV_WIDTH):
        conv = conv + cw_ref[w:w + 1, :] * ext_ref[pl.ds(8 - (CONV_WIDTH - 1) + w, cl), :]
    ext_ref[0:8, :] = xbc_ref[cl - 8:cl, :]
    xbc = conv * _sigmoid(conv)
    xs = xbc[:, :SSM_INNER]
    b_mat = xbc[:, SSM_INNER:SSM_INNER + SSM_GROUPS * SSM_STATE]
    c_mat = xbc[:, SSM_INNER + SSM_GROUPS * SSM_STATE:]

    dt = _softplus(dt_ref[...] + dtb_ref[...])
    log_a = dt * a_ref[...]
    row = lax.broadcasted_iota(jnp.int32, (cl, cl), 0)
    col = lax.broadcasted_iota(jnp.int32, (cl, cl), 1)
    causal = col <= row
    a_cum = jnp.dot(causal.astype(F32), log_a, precision=HIGHEST, preferred_element_type=F32)
    a_cum_t = a_cum.T
    expand = ex_ref[...]

    def widen(v):
        return jnp.dot(v, expand, precision=HIGHEST, preferred_element_type=F32)

    dt_w = widen(dt)
    a_cum_w = widen(a_cum)
    a_last_w = a_cum_w[cl - 1:cl, :]
    to_end_w = jnp.exp(a_last_w - a_cum_w)
    from_start_w = jnp.exp(a_cum_w)
    chunk_decay_w = jnp.exp(a_last_w)
    xc = xs * dt_w
    xw = (xc * to_end_w).astype(BF16)
    xc_b = xc.astype(BF16)
    lane = lax.broadcasted_iota(jnp.int32, (cl, LANES), 1)
    low_half = lane < SSM_HEAD_DIM

    y_parts = []
    for g in range(SSM_GROUPS):
        bg = b_mat[:, g * SSM_STATE:(g + 1) * SSM_STATE].astype(BF16)
        cg = c_mat[:, g * SSM_STATE:(g + 1) * SSM_STATE].astype(BF16)
        cb = lax.dot_general(cg, bg, (((1,), (1,)), ((), ())), preferred_element_type=F32)
        state = state_ref[g]
        y_off = jnp.dot(cg, state.astype(BF16), preferred_element_type=F32)
        y_off = y_off * from_start_w[:, g * gw:(g + 1) * gw]
        new_state = lax.dot_general(bg, xw[:, g * gw:(g + 1) * gw], (((0,), (0,)), ((), ())),
                                    preferred_element_type=F32)
        state_ref[g] = state * chunk_decay_w[:, g * gw:(g + 1) * gw] + new_state
        for pair in range(hg // 2):
            h0 = g * hg + 2 * pair
            c0 = h0 * SSM_HEAD_DIM
            x_pair = xc_b[:, c0:c0 + LANES]
            y_pair = None
            for k in range(2):
                h = h0 + k
                seg = a_cum[:, h:h + 1] - a_cum_t[h:h + 1, :]
                decay = jnp.where(causal, jnp.exp(jnp.where(causal, seg, 0.0)), 0.0)
                m = (cb * decay).astype(BF16)
                x_half = jnp.where(low_half if k == 0 else jnp.logical_not(low_half),
                                   x_pair, jnp.zeros_like(x_pair))
                part = jnp.dot(m, x_half, preferred_element_type=F32)
                y_pair = part if y_pair is None else y_pair + part
            y_parts.append(y_pair + y_off[:, c0 - g * gw:c0 - g * gw + LANES])
    y = jnp.concatenate(y_parts, axis=-1) + dsk_ref[...] * xs
    z = z_ref[...]
    yz = y * (z * _sigmoid(z))
    out = yz * lax.rsqrt(jnp.mean(yz * yz, axis=-1, keepdims=True) + EPS) * gn_ref[...]
    o_ref[...] = out.astype(o_ref.dtype)


def _ssd(z, xbc, misc, conv_w, conv_b, dt_bias_pad, a_pad, d_skip_w, g_norm, expand):
    t = z.shape[0]
    cl = SSD_CHUNK

    def full(shape):
        return pl.BlockSpec(shape, lambda i: (0,) * len(shape))

    return pl.pallas_call(
        functools.partial(_ssd_body, cl=cl),
        grid=(t // cl,),
        in_specs=[pl.BlockSpec((cl, SSM_INNER), lambda i: (i, 0)),
                  pl.BlockSpec((cl, SSM_CONV_DIM), lambda i: (i, 0)),
                  pl.BlockSpec((cl, LANES), lambda i: (i, 1)),
                  full((CONV_WIDTH, SSM_CONV_DIM)), full((1, SSM_CONV_DIM)),
                  full((1, LANES)), full((1, LANES)), full((1, SSM_INNER)),
                  full((1, SSM_INNER)), full((LANES, SSM_INNER))],
        out_specs=pl.BlockSpec((cl, SSM_INNER), lambda i: (i, 0)),
        out_shape=jax.ShapeDtypeStruct((t, SSM_INNER), BF16),
        scratch_shapes=[pltpu.VMEM((cl + 8, SSM_CONV_DIM), F32),
                        pltpu.VMEM((SSM_GROUPS, SSM_STATE, SSM_INNER // SSM_GROUPS), F32)],
        compiler_params=_params(("arbitrary",)),
        name="ssd_mixer",
    )(z, xbc, misc, conv_w, conv_b, dt_bias_pad, a_pad, d_skip_w, g_norm, expand)


def _merge_body(g0_ref, g1_ref, g2_ref, osb_ref, ofx_ref, osm_ref, wsb_ref, wfx_ref, wsm_ref, o_ref):
    merged = (g0_ref[...].astype(F32) * jnp.dot(osb_ref[...], wsb_ref[...], preferred_element_type=F32)
              + g1_ref[...].astype(F32) * jnp.dot(ofx_ref[...], wfx_ref[...], preferred_element_type=F32)
              + g2_ref[...].astype(F32) * jnp.dot(osm_ref[...], wsm_ref[...], preferred_element_type=F32))
    o_ref[...] = merged.astype(o_ref.dtype)


def _merge(gates, o_sb, o_fox, o_ssm, w_sb, w_fox, w_ssm):
    t = gates.shape[0]
    tm, tn = ROW_TILE, 512
    nj = D_MODEL // tn

    def gate_spec(b):
        return pl.BlockSpec((tm, tn), lambda j, i: (i, b * nj + j))

    def act_spec(width):
        return pl.BlockSpec((tm, width), lambda j, i: (i, 0))

    def w_spec(width):
        return pl.BlockSpec((width, tn), lambda j, i: (0, j))

    return pl.pallas_call(
        _merge_body,
        grid=(nj, t // tm),
        in_specs=[gate_spec(0), gate_spec(1), gate_spec(2),
                  act_spec(SB_WIDTH), act_spec(FOX_WIDTH), act_spec(SSM_INNER),
                  w_spec(SB_WIDTH), w_spec(FOX_WIDTH), w_spec(SSM_INNER)],
        out_specs=pl.BlockSpec((tm, tn), lambda j, i: (i, j)),
        out_shape=jax.ShapeDtypeStruct((t, D_MODEL), BF16),
        compiler_params=_params(("arbitrary", "arbitrary")),
        name="branch_merge",
    )(gates, gates, gates, o_sb, o_fox, o_ssm, w_sb, w_fox, w_ssm)


def _route(logits_t, b_router, count_ref, tm):
    aff = _sigmoid(logits_t)
    sel = aff + b_router
    rows = [sel[e:e + 1, :] for e in range(N_EXPERTS)]
    best_score, best_group = None, None
    for g in range(N_EXPERT_GROUPS):
        r = rows[g * EXPERTS_PER_GROUP:(g + 1) * EXPERTS_PER_GROUP]
        score = None
        for a in range(EXPERTS_PER_GROUP):
            for b in range(a + 1, EXPERTS_PER_GROUP):
                pair = r[a] + r[b]
                score = pair if score is None else jnp.maximum(score, pair)
        if g == 0:
            best_score, best_group = score, jnp.zeros_like(score, dtype=jnp.int32)
        else:
            better = score > best_score
            best_group = jnp.where(better, g, best_group)
            best_score = jnp.where(better, score, best_score)
    masked = [jnp.where(best_group == e // EXPERTS_PER_GROUP, rows[e], -jnp.inf)
              for e in range(N_EXPERTS)]

    def arg_top(vals):
        top, idx = vals[0], jnp.zeros_like(best_group)
        for e in range(1, N_EXPERTS):
            better = vals[e] > top
            idx = jnp.where(better, e, idx)
            top = jnp.where(better, vals[e], top)
        return idx

    e1 = arg_top(masked)
    e2 = arg_top([jnp.where(e1 == e, -jnp.inf, masked[e]) for e in range(N_EXPERTS)])
    expert_id = lax.broadcasted_iota(jnp.int32, (N_EXPERTS, tm), 0)
    pick1 = expert_id == e1
    pick2 = expert_id == e2
    a1 = jnp.sum(jnp.where(pick1, aff, 0.0), axis=0, keepdims=True)
    a2 = jnp.sum(jnp.where(pick2, aff, 0.0), axis=0, keepdims=True)
    denom = a1 + a2
    chosen = jnp.logical_or(pick1, pick2)
    t_row = lax.broadcasted_iota(jnp.int32, (tm, tm), 0)
    t_col = lax.broadcasted_iota(jnp.int32, (tm, tm), 1)
    earlier = (t_row < t_col).astype(BF16)
    rank = jnp.dot(chosen.astype(BF16), earlier, preferred_element_type=F32) + count_ref[...]
    r1 = jnp.sum(jnp.where(pick1, rank, 0.0), axis=0, keepdims=True)
    r2 = jnp.sum(jnp.where(pick2, rank, 0.0), axis=0, keepdims=True)
    count_ref[...] = count_ref[...] + jnp.sum(chosen.astype(F32), axis=1, keepdims=True)
    zero = jnp.zeros_like(a1)
    return jnp.concatenate([e1.astype(F32), e2.astype(F32), r1, r2, a1 / denom, a2 / denom,
                            zero, zero], axis=0)


def _outproj_body(m_ref, w_ref, x_ref, gm_ref, g_ref, sc_ref, sh_ref, wr_ref, br_ref,
                  xo_ref, hp_ref, rt_ref, cnt_ref, count_ref, *, tm):
    @pl.when(pl.program_id(0) == 0)
    def _():
        count_ref[...] = jnp.zeros_like(count_ref)

    x_new = x_ref[...] + gm_ref[...] * jnp.dot(m_ref[...], w_ref[...], preferred_element_type=F32)
    xo_ref[...] = x_new
    h = _modnorm(x_new, g_ref[...], sc_ref[...], sh_ref[...])
    hp_ref[...] = h
    logits_t = lax.dot_general(wr_ref[...], h, (((1,), (1,)), ((), ())),
                               precision=HIGHEST, preferred_element_type=F32)
    rt_ref[...] = _route(logits_t, br_ref[...], count_ref, tm)
    cnt_ref[...] = jnp.broadcast_to(count_ref[...], cnt_ref.shape)


def _outproj_route(merged, w_out, x, gate_m, g_ffn, scale_f, shift_f, w_router_t, b_router_col):
    t, d = x.shape
    tm = ROW_TILE
    vec = pl.BlockSpec((1, d), lambda i: (0, 0))
    return pl.pallas_call(
        functools.partial(_outproj_body, tm=tm),
        grid=(t // tm,),
        in_specs=[pl.BlockSpec((tm, d), lambda i: (i, 0)),
                  pl.BlockSpec((d, d), lambda i: (0, 0)),
                  pl.BlockSpec((tm, d), lambda i: (i, 0)),
                  vec, vec, vec, vec,
                  pl.BlockSpec((N_EXPERTS, d), lambda i: (0, 0)),
                  pl.BlockSpec((N_EXPERTS, 1), lambda i: (0, 0))],
        out_specs=[pl.BlockSpec((tm, d), lambda i: (i, 0)),
                   pl.BlockSpec((tm, d), lambda i: (i, 0)),
                   pl.BlockSpec((8, tm), lambda i: (0, i)),
                   pl.BlockSpec((N_EXPERTS, LANES), lambda i: (0, 0))],
        out_shape=[jax.ShapeDtypeStruct((t, d), F32),
                   jax.ShapeDtypeStruct((t, d), F32),
                   jax.ShapeDtypeStruct((8, t), F32),
                   jax.ShapeDtypeStruct((N_EXPERTS, LANES), F32)],
        scratch_shapes=[pltpu.VMEM((N_EXPERTS, 1), F32)],
        compiler_params=_params(("arbitrary",)),
        name="outproj_route",
    )(merged, w_out, x, gate_m, g_ffn, scale_f, shift_f, w_router_t, b_router_col)


def _dispatch_body(dest_ref, h_ref, init_ref, xs_ref, sem, *, tm, t):
    del init_ref
    base = pl.program_id(0) * tm

    def row_copy(r, d):
        return pltpu.make_async_copy(h_ref.at[pl.ds(r, 1)], xs_ref.at[pl.ds(d, 1)], sem)

    def issue(r, carry):
        row_copy(r, dest_ref[base + r]).start()
        row_copy(r, dest_ref[t + base + r]).start()
        return carry

    lax.fori_loop(0, tm, issue, 0)

    def drain(r, carry):
        row_copy(r, 0).wait()
        row_copy(r, 0).wait()
        return carry

    lax.fori_loop(0, tm, drain, 0)


def _dispatch(dest, h, n_rows):
    t, d = h.shape
    tm = ROW_TILE
    init = jnp.zeros((n_rows, d), F32)
    return pl.pallas_call(
        functools.partial(_dispatch_body, tm=tm, t=t),
        grid_spec=pltpu.PrefetchScalarGridSpec(
            num_scalar_prefetch=1,
            grid=(t // tm,),
            in_specs=[pl.BlockSpec((tm, d), lambda i, dest: (i, 0)),
                      pl.BlockSpec(memory_space=pl.ANY)],
            out_specs=pl.BlockSpec(memory_space=pl.ANY),
            scratch_shapes=[pltpu.SemaphoreType.DMA(())]),
        out_shape=jax.ShapeDtypeStruct((n_rows, d), F32),
        input_output_aliases={2: 0},
        compiler_params=_params(("arbitrary",)),
        name="moe_dispatch",
    )(dest, h, init)


def _expert_body(te_ref, nu_ref, xs_ref, wg_ref, wu_ref, wd_ref, o_ref):
    del te_ref

    @pl.when(pl.program_id(0) < nu_ref[0])
    def _():
        xb = xs_ref[...].astype(BF16)
        gate = jnp.dot(xb, wg_ref[...], preferred_element_type=F32)
        up = jnp.dot(xb, wu_ref[...], preferred_element_type=F32)
        act = (gate * _sigmoid(gate) * up).astype(BF16)
        o_ref[...] = jnp.dot(act, wd_ref[...], preferred_element_type=F32)

    @pl.when(pl.program_id(0) >= nu_ref[0])
    def _():
        o_ref[...] = jnp.zeros_like(o_ref)


def _experts(tile_expert, n_used, xs, w_gate, w_up, w_down):
    n_rows = xs.shape[0]
    te = EXPERT_TILE

    def row_map(i, tex, nu):
        return (jnp.minimum(i, nu[0] - 1), 0)

    def w_map(i, tex, nu):
        return (tex[i], 0, 0)

    return pl.pallas_call(
        _expert_body,
        grid_spec=pltpu.PrefetchScalarGridSpec(
            num_scalar_prefetch=2,
            grid=(n_rows // te,),
            in_specs=[pl.BlockSpec((te, D_MODEL), row_map),
                      pl.BlockSpec((None, D_MODEL, D_FF_EXPERT), w_map),
                      pl.BlockSpec((None, D_MODEL, D_FF_EXPERT), w_map),
                      pl.BlockSpec((None, D_FF_EXPERT, D_MODEL), w_map)],
            out_specs=pl.BlockSpec((te, D_MODEL), lambda i, tex, nu: (i, 0))),
        out_shape=jax.ShapeDtypeStruct((n_rows, D_MODEL), F32),
        compiler_params=_params(("arbitrary",)),
        name="moe_experts",
    )(tile_expert, n_used, xs, w_gate, w_up, w_down)


def _combine_body(dest_ref, ys_ref, x_ref, w_ref, gf_ref, o_ref, a_buf, b_buf, sem, *, tm, t):
    base = pl.program_id(0) * tm

    def row_copy(d, buf, r):
        return pltpu.make_async_copy(ys_ref.at[pl.ds(d, 1)], buf.at[pl.ds(r, 1)], sem)

    def issue(r, carry):
        row_copy(dest_ref[base + r], a_buf, r).start()
        row_copy(dest_ref[t + base + r], b_buf, r).start()
        return carry

    lax.fori_loop(0, tm, issue, 0)

    def drain(r, carry):
        row_copy(0, a_buf, r).wait()
        row_copy(0, b_buf, r).wait()
        return carry

    lax.fori_loop(0, tm, drain, 0)
    w = w_ref[...]
    y = w[:, 0:1] * a_buf[...] + w[:, 1:2] * b_buf[...]
    o_ref[...] = x_ref[...] + gf_ref[...] * y


def _combine(dest, ys, x, w_cols, gate_f):
    t, d = x.shape
    tm = COMBINE_TILE
    return pl.pallas_call(
        functools.partial(_combine_body, tm=tm, t=t),
        grid_spec=pltpu.PrefetchScalarGridSpec(
            num_scalar_prefetch=1,
            grid=(t // tm,),
            in_specs=[pl.BlockSpec(memory_space=pl.ANY),
                      pl.BlockSpec((tm, d), lambda i, dest: (i, 0)),
                      pl.BlockSpec((tm, 8), lambda i, dest: (i, 0)),
                      pl.BlockSpec((1, d), lambda i, dest: (0, 0))],
            out_specs=pl.BlockSpec((tm, d), lambda i, dest: (i, 0)),
            scratch_shapes=[pltpu.VMEM((tm, d), F32), pltpu.VMEM((tm, d), F32),
                            pltpu.SemaphoreType.DMA(())]),
        out_shape=jax.ShapeDtypeStruct((t, d), F32),
        compiler_params=_params(("arbitrary",)),
        name="moe_combine",
    )(dest, ys, x, w_cols, gate_f)


def _pad_lanes(v, fill=0.0):
    return jnp.pad(v.astype(F32), (0, LANES - v.shape[0]), constant_values=fill).reshape(1, LANES)


def _routing_tables(route, counts, t):
    te = EXPERT_TILE
    n_tiles = (2 * t) // te + N_EXPERTS
    cnt = counts[:, 0].astype(jnp.int32)
    padded = ((cnt + te - 1) // te) * te
    ends = jnp.cumsum(padded)
    offsets = ends - padded
    e1 = route[0].astype(jnp.int32)
    e2 = route[1].astype(jnp.int32)
    d1 = offsets[e1] + route[2].astype(jnp.int32)
    d2 = offsets[e2] + route[3].astype(jnp.int32)
    dest = jnp.concatenate([d1, d2]).astype(jnp.int32)
    n_used = (ends[-1] // te).astype(jnp.int32)
    tile_start = jnp.arange(n_tiles, dtype=jnp.int32) * te
    tile_expert = jnp.sum((tile_start[:, None] >= ends[None, :]).astype(jnp.int32), axis=1)
    last_expert = tile_expert[jnp.maximum(n_used - 1, 0)]
    tile_expert = jnp.where(jnp.arange(n_tiles) < n_used, tile_expert, last_expert)
    tile_expert = jnp.minimum(tile_expert, N_EXPERTS - 1).astype(jnp.int32)
    w_cols = jnp.transpose(route)
    w_cols = jnp.concatenate([w_cols[:, 4:6], w_cols[:, 0:6]], axis=1)
    return dest, tile_expert, n_used.reshape(1), w_cols, n_tiles * te


def kernel(x, c, w_ada, b_ada, g_norm_mix, w_in, b_fgate, g_q_fox, g_k_fox, conv_w, conv_b,
           dt_bias, a_log, d_skip, g_ssm_norm, w_branch_sb, w_branch_fox, w_branch_ssm, w_out,
           g_norm_ffn, w_router, b_router, w_e_gate, w_e_up, w_e_down):
    bsz, t, d = x.shape
    assert bsz == 1 and d == D_MODEL
    n_layers = w_ada.shape[0]
    xt = x.reshape(t, d)

    mod = _ada(c, w_ada, b_ada)
    expand = (jnp.arange(SSM_INNER)[None, :] // SSM_HEAD_DIM
              == jnp.arange(LANES)[:, None]).astype(F32)
    w_router_t = jnp.transpose(w_router)
    b_router_col = b_router.reshape(N_EXPERTS, 1)

    o_qkv = 0
    o_ff = 3 * SB_WIDTH + 3 * FOX_WIDTH
    o_z = o_ff + FOX_HEADS
    o_xbc = o_z + SSM_INNER
    o_dt = o_xbc + SSM_CONV_DIM
    o_gate = o_dt + SSM_HEADS
    q_scale = jnp.concatenate([jnp.full((SB_WIDTH,), HEAD_DIM ** -0.5 * LOG2E, F32),
                               jnp.ones((o_ff - SB_WIDTH,), F32)])

    for layer in range(n_layers):
        m = mod[layer].reshape(6, 1, d)
        shift_m, scale_m, gate_m, shift_f, scale_f, gate_f = (m[i] for i in range(6))
        wl = w_in[layer]
        w_qkv = (wl[:, o_qkv:o_ff] * q_scale[None, :]).astype(BF16)
        w_z = wl[:, o_z:o_xbc].astype(BF16)
        w_xbc = wl[:, o_xbc:o_dt].astype(BF16)
        w_misc = jnp.concatenate(
            [jnp.pad(wl[:, o_ff:o_z], ((0, 0), (0, LANES - FOX_HEADS))),
             jnp.pad(wl[:, o_dt:o_gate], ((0, 0), (0, LANES - SSM_HEADS)))], axis=1).astype(BF16)
        w_gates = wl[:, o_gate:].astype(BF16)

        h = _norm(xt, g_norm_mix[layer].reshape(1, d), scale_m, shift_m)
        qkv = _mm(h, w_qkv, BF16, 1024, name="in_proj_qkv")
        z = _mm(h, w_z, F32, 1024, name="in_proj_z")
        xbc = _mm(h, w_xbc, F32, 768, name="in_proj_xbc")
        misc = _mm(h, w_misc, F32, 256, name="in_proj_misc")
        gates = _mm(h, w_gates, BF16, 1024, act="sigmoid", name="in_proj_gates")

        o_sb = _sb_attention(qkv)
        qn, kn, f_col = _fox_prep(qkv, misc, _pad_lanes(b_fgate[layer]),
                                  g_q_fox[layer].reshape(1, HEAD_DIM),
                                  g_k_fox[layer].reshape(1, HEAD_DIM))
        f_row = jnp.transpose(f_col[:, :8])
        o_fox = _fox_attention(qn, kn, qkv, f_col, f_row)
        o_ssm = _ssd(z, xbc, misc, conv_w[layer], conv_b[layer].reshape(1, SSM_CONV_DIM),
                     _pad_lanes(dt_bias[layer]), _pad_lanes(-jnp.exp(a_log[layer].astype(F32))),
                     jnp.repeat(d_skip[layer].astype(F32), SSM_HEAD_DIM).reshape(1, SSM_INNER),
                     g_ssm_norm[layer].reshape(1, SSM_INNER), expand)

        merged = _merge(gates, o_sb, o_fox, o_ssm, w_branch_sb[layer].astype(BF16),
                        w_branch_fox[layer].astype(BF16), w_branch_ssm[layer].astype(BF16))
        xt, h_moe, route, counts = _outproj_route(
            merged, w_out[layer].astype(BF16), xt, gate_m, g_norm_ffn[layer].reshape(1, d),
            scale_f, shift_f, w_router_t, b_router_col)

        dest, tile_expert, n_used, w_cols, n_rows = _routing_tables(route, counts, t)
        xs = _dispatch(dest, h_moe, n_rows)
        ys = _experts(tile_expert, n_used, xs, w_e_gate[layer].astype(BF16),
                      w_e_up[layer].astype(BF16), w_e_down[layer].astype(BF16))
        xt = _combine(dest, ys, xt, w_cols, gate_f)
    return xt.reshape(bsz, t, d)
```

```python
import functools

import jax
import jax.numpy as jnp
from jax import lax
from jax.experimental import pallas as pl
from jax.experimental.pallas import tpu as pltpu

F32 = jnp.float32
BF16 = jnp.bfloat16
HIGHEST = lax.Precision.HIGHEST
LOG2E = 1.4426950408889634

D_MODEL = 2048
EPS = 1e-6
HEAD_DIM = 128
SB_HEADS = 4
FOX_HEADS = 4
SB_WIDTH = SB_HEADS * HEAD_DIM
FOX_WIDTH = FOX_HEADS * HEAD_DIM
SSM_HEAD_DIM = 64
SSM_HEADS = 16
SSM_INNER = SSM_HEADS * SSM_HEAD_DIM
SSM_GROUPS = 2
SSM_STATE = 128
CONV_WIDTH = 4
SSM_CONV_DIM = SSM_INNER + 2 * SSM_GROUPS * SSM_STATE
N_BRANCH = 3
N_EXPERTS = 16
N_EXPERT_GROUPS = 4
EXPERTS_PER_GROUP = N_EXPERTS // N_EXPERT_GROUPS
D_FF_EXPERT = 1024

LANES = 128
V7X_VMEM_BYTES = 64 * 1024 * 1024
VMEM_LIMIT = 48 * 1024 * 1024

ROW_TILE = 512
IN_TILE = 1024
ATTN_TILE = 256
ATTN_SUB = 4
SSD_CHUNK = 128
EXPERT_TILE = 256
COMBINE_TILE = 256


def _params(sem, vmem=VMEM_LIMIT):
    return pltpu.CompilerParams(dimension_semantics=sem, vmem_limit_bytes=vmem)


def _sigmoid(x):
    return 1.0 / (1.0 + jnp.exp(-x))


def _log_sigmoid(x):
    return jnp.minimum(x, 0.0) - jnp.log(1.0 + jnp.exp(-jnp.abs(x)))


def _softplus(x):
    return jnp.maximum(x, 0.0) + jnp.log(1.0 + jnp.exp(-jnp.abs(x)))


def _ada_body(c_ref, w_ref, b_ref, o_ref):
    k = pl.program_id(1)

    @pl.when(k == 0)
    def _():
        o_ref[0] = b_ref[0]

    c = c_ref[...]
    cond = c * _sigmoid(c)
    parts = [jnp.sum(w_ref[0, :, j * LANES:(j + 1) * LANES] * cond, axis=0, keepdims=True)
             for j in range(w_ref.shape[2] // LANES)]
    o_ref[0] += jnp.concatenate(parts, axis=1)


def _ada(c, w_ada, b_ada):
    n_layers, d, n = w_ada.shape
    tk = 256
    c_lanes = jnp.broadcast_to(c.reshape(d, 1), (d, LANES))
    out = pl.pallas_call(
        _ada_body,
        grid=(n_layers, d // tk),
        in_specs=[pl.BlockSpec((tk, LANES), lambda l, k: (k, 0)),
                  pl.BlockSpec((1, tk, n), lambda l, k: (l, k, 0)),
                  pl.BlockSpec((1, 1, n), lambda l, k: (l, 0, 0))],
        out_specs=pl.BlockSpec((1, 1, n), lambda l, k: (l, 0, 0)),
        out_shape=jax.ShapeDtypeStruct((n_layers, 1, n), F32),
        compiler_params=_params(("arbitrary", "arbitrary")),
        name="ada_mod",
    )(c_lanes, w_ada, b_ada.reshape(n_layers, 1, n))
    return out[:, 0, :]


def _modnorm(x, g, scale, shift):
    y = x * lax.rsqrt(jnp.mean(x * x, axis=-1, keepdims=True) + EPS) * g
    return y * (1.0 + scale) + shift


def _norm_body(x_ref, g_ref, sc_ref, sh_ref, o_ref):
    o_ref[...] = _modnorm(x_ref[...], g_ref[...], sc_ref[...], sh_ref[...]).astype(BF16)


def _norm(x, g, scale, shift):
    t, d = x.shape
    vec = pl.BlockSpec((1, d), lambda i: (0, 0))
    return pl.pallas_call(
        _norm_body,
        grid=(t // ROW_TILE,),
        in_specs=[pl.BlockSpec((ROW_TILE, d), lambda i: (i, 0)), vec, vec, vec],
        out_specs=pl.BlockSpec((ROW_TILE, d), lambda i: (i, 0)),
        out_shape=jax.ShapeDtypeStruct((t, d), BF16),
        compiler_params=_params(("arbitrary",)),
        name="mod_norm",
    )(x, g, scale, shift)


def _mm_body(a_ref, w_ref, o_ref, *, act):
    r = jnp.dot(a_ref[...], w_ref[...], preferred_element_type=F32)
    if act == "sigmoid":
        r = _sigmoid(r)
    o_ref[...] = r.astype(o_ref.dtype)


def _mm(a, w, out_dtype, tn, act=None, tm=1024, name="matmul"):
    m, k = a.shape
    n = w.shape[1]
    tm = min(tm, m)
    return pl.pallas_call(
        functools.partial(_mm_body, act=act),
        grid=(m // tm, n // tn),
        in_specs=[pl.BlockSpec((tm, k), lambda i, j: (i, 0)),
                  pl.BlockSpec((k, tn), lambda i, j: (0, j))],
        out_specs=pl.BlockSpec((tm, tn), lambda i, j: (i, j)),
        out_shape=jax.ShapeDtypeStruct((m, n), out_dtype),
        compiler_params=_params(("arbitrary", "arbitrary")),
        name=name,
    )(a, w)


def _cast_body(w_ref, o_ref):
    o_ref[...] = w_ref[...].astype(o_ref.dtype)


def _cast_bf16(w):
    shape = w.shape
    n = shape[-1]
    rows = w.size // n
    tr = min(rows, (2 * 1024 * 1024) // n)
    out = pl.pallas_call(
        _cast_body,
        grid=(rows // tr,),
        in_specs=[pl.BlockSpec((tr, n), lambda i: (i, 0))],
        out_specs=pl.BlockSpec((tr, n), lambda i: (i, 0)),
        out_shape=jax.ShapeDtypeStruct((rows, n), BF16),
        compiler_params=_params(("arbitrary",)),
        name="cast_bf16",
    )(w.reshape(rows, n))
    return out.reshape(shape)


def _proj_body(*refs, act, scaled):
    if scaled:
        h_ref, w_ref, sc_ref, o_ref, wb_ref = refs
    else:
        h_ref, w_ref, o_ref, wb_ref = refs

    @pl.when(pl.program_id(1) == 0)
    def _():
        wb_ref[...] = w_ref[0].astype(BF16)

    r = jnp.dot(h_ref[...], wb_ref[...], preferred_element_type=F32)
    if scaled:
        r = r * sc_ref[...]
    if act == "sigmoid":
        r = _sigmoid(r)
    o_ref[...] = r.astype(o_ref.dtype)


def _proj(h, w_in, layer, first_tile, n_tiles, out_dtype, act=None, col_scale=None, name="in_proj"):
    t, k = h.shape
    tm, tn = 1024, IN_TILE
    assert (first_tile + n_tiles) * tn <= w_in.shape[2]
    in_specs = [pl.BlockSpec((tm, k), lambda j, i: (i, 0)),
                pl.BlockSpec((1, k, tn), lambda j, i: (layer, 0, first_tile + j))]
    args = [h, w_in]
    if col_scale is not None:
        in_specs.append(pl.BlockSpec((1, tn), lambda j, i: (0, j)))
        args.append(col_scale)
    return pl.pallas_call(
        functools.partial(_proj_body, act=act, scaled=col_scale is not None),
        grid=(n_tiles, t // tm),
        in_specs=in_specs,
        out_specs=pl.BlockSpec((tm, tn), lambda j, i: (i, j)),
        out_shape=jax.ShapeDtypeStruct((t, n_tiles * tn), out_dtype),
        scratch_shapes=[pltpu.VMEM((k, tn), BF16)],
        compiler_params=_params(("arbitrary", "arbitrary")),
        name=name,
    )(*args)


def _fox_prep_body(q_ref, k_ref, f_ref, bf_ref, gq_ref, gk_ref,
                   qo_ref, ko_ref, fo_ref, carry_ref, *, tm):
    @pl.when(pl.program_id(0) == 0)
    def _():
        carry_ref[...] = jnp.zeros_like(carry_ref)

    scale = HEAD_DIM ** -0.5 * LOG2E
    for h in range(FOX_HEADS):
        sl = slice(h * HEAD_DIM, (h + 1) * HEAD_DIM)
        q = q_ref[:, sl].astype(F32)
        qn = q * lax.rsqrt(jnp.mean(q * q, axis=-1, keepdims=True) + EPS) * gq_ref[...]
        qo_ref[:, sl] = (qn * scale).astype(BF16)
        k = k_ref[:, sl].astype(F32)
        kn = k * lax.rsqrt(jnp.mean(k * k, axis=-1, keepdims=True) + EPS) * gk_ref[...]
        ko_ref[:, sl] = kn.astype(BF16)

    log_f = _log_sigmoid(f_ref[...] + bf_ref[...])
    row = lax.broadcasted_iota(jnp.int32, (tm, tm), 0)
    col = lax.broadcasted_iota(jnp.int32, (tm, tm), 1)
    tri = (col <= row).astype(F32)
    cum = jnp.dot(tri, log_f, precision=HIGHEST, preferred_element_type=F32) + carry_ref[...]
    fo_ref[...] = cum * LOG2E
    carry_ref[...] = cum[tm - 1:tm, :]


def _fox_prep(qkv, misc, b_fgate_pad, g_q, g_k):
    t = qkv.shape[0]
    tm = ROW_TILE
    q_blk = 3 * SB_WIDTH // FOX_WIDTH
    vec = pl.BlockSpec((1, LANES), lambda i: (0, 0))
    return pl.pallas_call(
        functools.partial(_fox_prep_body, tm=tm),
        grid=(t // tm,),
        in_specs=[pl.BlockSpec((tm, FOX_WIDTH), lambda i: (i, q_blk)),
                  pl.BlockSpec((tm, FOX_WIDTH), lambda i: (i, q_blk + 1)),
                  pl.BlockSpec((tm, LANES), lambda i: (i, 0)),
                  vec, vec, vec],
        out_specs=[pl.BlockSpec((tm, FOX_WIDTH), lambda i: (i, 0)),
                   pl.BlockSpec((tm, FOX_WIDTH), lambda i: (i, 0)),
                   pl.BlockSpec((tm, LANES), lambda i: (i, 0))],
        out_shape=[jax.ShapeDtypeStruct((t, FOX_WIDTH), BF16),
                   jax.ShapeDtypeStruct((t, FOX_WIDTH), BF16),
                   jax.ShapeDtypeStruct((t, LANES), F32)],
        scratch_shapes=[pltpu.VMEM((1, LANES), F32)],
        compiler_params=_params(("arbitrary",)),
        name="fox_prep",
    )(qkv, qkv, misc, b_fgate_pad, g_q, g_k)


def _sb_body(q_ref, k_ref, v_ref, o_ref, *, tb, nsub):
    qb = pl.program_id(1)
    qs = [q_ref[a * tb:(a + 1) * tb, :] for a in range(nsub)]
    row = lax.broadcasted_iota(jnp.int32, (tb, tb), 0)
    col = lax.broadcasted_iota(jnp.int32, (tb, tb), 1)
    ones = (row >= col).astype(BF16)
    suffix_ones = jnp.concatenate([ones, ones], axis=0)
    past = col < row

    def load(kb):
        ks = pl.multiple_of(kb * tb, tb)
        return k_ref[pl.ds(ks, tb), :], v_ref[pl.ds(ks, tb), :]

    def sweep(k, v, chains):
        us = [lax.dot_general(q, k, (((1,), (1,)), ((), ())), preferred_element_type=F32)
              for q, _, _, _ in chains]
        splits = []
        for u, (_, _, _, diagonal) in zip(us, chains):
            drop = jnp.maximum(u, 0.0) + jnp.log2(1.0 + jnp.exp2(-jnp.abs(u)))
            if diagonal:
                drop = jnp.where(past, drop, 0.0)
            hi = drop.astype(BF16)
            lo = (drop - hi.astype(F32)).astype(BF16)
            splits.append(jnp.concatenate([hi, lo], axis=1))
        sufs = [jnp.dot(s, suffix_ones, preferred_element_type=F32) for s in splits]
        ws = []
        for u, suf, (_, carry, _, diagonal) in zip(us, sufs, chains):
            w = jnp.exp2(u - suf - carry)
            if diagonal:
                w = jnp.where(past, w, 0.0)
            ws.append(w.astype(BF16))
        return [(carry + suf[:, 0:1], acc + jnp.dot(w, v, preferred_element_type=F32))
                for w, suf, (_, carry, acc, _) in zip(ws, sufs, chains)]

    state = [(jnp.zeros((tb, 1), F32), jnp.zeros((tb, HEAD_DIM), F32)) for _ in range(nsub)]
    for j in reversed(range(nsub)):
        k, v = load(qb * nsub + j)
        new = sweep(k, v, [(qs[a], state[a][0], state[a][1], a == j) for a in range(j, nsub)])
        state[j:] = new

    def step(i, c):
        k, v = load(qb * nsub - 1 - i)
        return tuple(sweep(k, v, [(qs[a], c[a][0], c[a][1], False) for a in range(nsub)]))

    state = lax.fori_loop(0, qb * nsub, step, tuple(state))
    for a in range(nsub):
        o_ref[a * tb:(a + 1) * tb, :] = state[a][1].astype(o_ref.dtype)


def _sb_attention(qkv):
    t = qkv.shape[0]
    tb, nsub = ATTN_TILE, ATTN_SUB
    return pl.pallas_call(
        functools.partial(_sb_body, tb=tb, nsub=nsub),
        grid=(SB_HEADS, t // (tb * nsub)),
        in_specs=[pl.BlockSpec((tb * nsub, HEAD_DIM), lambda h, i: (i, h)),
                  pl.BlockSpec((t, HEAD_DIM), lambda h, i: (0, SB_HEADS + h)),
                  pl.BlockSpec((t, HEAD_DIM), lambda h, i: (0, 2 * SB_HEADS + h))],
        out_specs=pl.BlockSpec((tb * nsub, HEAD_DIM), lambda h, i: (i, h)),
        out_shape=jax.ShapeDtypeStruct((t, SB_WIDTH), BF16),
        compiler_params=_params(("arbitrary", "arbitrary")),
        name="sb_attention",
    )(qkv, qkv, qkv)


def _fox_body(q_ref, k_ref, v_ref, fcol_ref, frow_ref, o_ref, *, tb, nsub):
    h = pl.program_id(0)
    qb = pl.program_id(1)
    lane = lax.broadcasted_iota(jnp.int32, (tb, LANES), 1)
    qs, f_qs = [], []
    for a in range(nsub):
        qs.append(q_ref[a * tb:(a + 1) * tb, :])
        f_qs.append(jnp.sum(jnp.where(lane == h, fcol_ref[a * tb:(a + 1) * tb, :], 0.0),
                            axis=-1, keepdims=True))
    row = lax.broadcasted_iota(jnp.int32, (tb, tb), 0)
    col = lax.broadcasted_iota(jnp.int32, (tb, tb), 1)
    causal = col <= row

    def load(kb):
        ks = pl.multiple_of(kb * tb, tb)
        return (k_ref[pl.ds(ks, tb), :], v_ref[pl.ds(ks, tb), :],
                frow_ref[pl.ds(h, 1), pl.ds(ks, tb)])

    def sweep(k, v, f_k, chains):
        ss = []
        for a, _, diagonal in chains:
            s = lax.dot_general(qs[a], k, (((1,), (1,)), ((), ())), preferred_element_type=F32) - f_k
            ss.append(jnp.where(causal, s, -jnp.inf) if diagonal else s)
        ms, ps, sums = [], [], []
        for s, (a, (m, _, _), _) in zip(ss, chains):
            m_new = jnp.maximum(m, jnp.max(s, axis=-1, keepdims=True) + f_qs[a])
            p = jnp.exp2(s - (m_new - f_qs[a]))
            ms.append(m_new)
            sums.append(jnp.sum(p, axis=-1, keepdims=True))
            ps.append(p.astype(BF16))
        out = []
        for p, p_sum, m_new, (_, (m, l, acc), _) in zip(ps, sums, ms, chains):
            alpha = jnp.exp2(m - m_new)
            out.append((m_new, alpha * l + p_sum,
                        alpha * acc + jnp.dot(p, v, preferred_element_type=F32)))
        return out

    def step(kb, c):
        k, v, f_k = load(kb)
        return tuple(sweep(k, v, f_k, [(a, c[a], False) for a in range(nsub)]))

    init = tuple((jnp.full((tb, 1), -jnp.inf, F32), jnp.zeros((tb, 1), F32),
                  jnp.zeros((tb, HEAD_DIM), F32)) for _ in range(nsub))
    state = list(lax.fori_loop(0, qb * nsub, step, init))
    for j in range(nsub):
        k, v, f_k = load(qb * nsub + j)
        state[j:] = sweep(k, v, f_k, [(a, state[a], a == j) for a in range(j, nsub)])
    for a in range(nsub):
        o_ref[a * tb:(a + 1) * tb, :] = (state[a][2] / state[a][1]).astype(o_ref.dtype)


def _fox_attention(qn, kn, qkv, f_col, f_row):
    t = qn.shape[0]
    tb, nsub = ATTN_TILE, ATTN_SUB
    v_blk = (3 * SB_WIDTH + 2 * FOX_WIDTH) // HEAD_DIM
    return pl.pallas_call(
        functools.partial(_fox_body, tb=tb, nsub=nsub),
        grid=(FOX_HEADS, t // (tb * nsub)),
        in_specs=[pl.BlockSpec((tb * nsub, HEAD_DIM), lambda h, i: (i, h)),
                  pl.BlockSpec((t, HEAD_DIM), lambda h, i: (0, h)),
                  pl.BlockSpec((t, HEAD_DIM), lambda h, i: (0, v_blk + h)),
                  pl.BlockSpec((tb * nsub, LANES), lambda h, i: (i, 0)),
                  pl.BlockSpec((8, t), lambda h, i: (0, 0))],
        out_specs=pl.BlockSpec((tb * nsub, HEAD_DIM), lambda h, i: (i, h)),
        out_shape=jax.ShapeDtypeStruct((t, FOX_WIDTH), BF16),
        compiler_params=_params(("arbitrary", "arbitrary")),
        name="fox_attention",
    )(qn, kn, qkv, f_col, f_row)


def _ssd_body(mid_ref, cw_ref, cb_ref, dtb_ref, a_ref, dsk_ref, gn_ref, ex_ref,
              o_ref, ext_ref, state_ref, *, cl, off_z, off_xbc, off_dt):
    hg = SSM_HEADS // SSM_GROUPS
    gw = hg * SSM_HEAD_DIM

    @pl.when(pl.program_id(0) == 0)
    def _():
        ext_ref[0:8, :] = jnp.zeros((8, SSM_CONV_DIM), F32)
        state_ref[...] = jnp.zeros_like(state_ref)

    z = mid_ref[:, off_z:off_z + SSM_INNER]
    xbc_raw = mid_ref[:, off_xbc:off_xbc + SSM_CONV_DIM]
    lane = lax.broadcasted_iota(jnp.int32, (cl, LANES), 1)
    dt_raw = jnp.where(lane < SSM_HEADS, mid_ref[:, off_dt:off_dt + LANES], 0.0)

    ext_ref[8:8 + cl, :] = xbc_raw
    conv = cb_ref[...] + cw_ref[0:1, :] * ext_ref[pl.ds(8 - (CONV_WIDTH - 1), cl), :]
    for w in range(1, CONV_WIDTH):
        conv = conv + cw_ref[w:w + 1, :] * ext_ref[pl.ds(8 - (CONV_WIDTH - 1) + w, cl), :]
    ext_ref[0:8, :] = xbc_raw[cl - 8:cl, :]
    xbc = conv * _sigmoid(conv)
    xs = xbc[:, :SSM_INNER]
    b_mat = xbc[:, SSM_INNER:SSM_INNER + SSM_GROUPS * SSM_STATE]
    c_mat = xbc[:, SSM_INNER + SSM_GROUPS * SSM_STATE:]

    dt = _softplus(dt_raw + dtb_ref[...])
    log_a = dt * a_ref[...]
    row = lax.broadcasted_iota(jnp.int32, (cl, cl), 0)
    col = lax.broadcasted_iota(jnp.int32, (cl, cl), 1)
    causal = col <= row
    a_cum = jnp.dot(causal.astype(F32), log_a, precision=HIGHEST, preferred_element_type=F32)
    a_cum_t = a_cum.T
    expand = ex_ref[...]

    def widen(v):
        return jnp.dot(v, expand, precision=HIGHEST, preferred_element_type=F32)

    dt_w = widen(dt)
    a_cum_w = widen(a_cum)
    a_last_w = a_cum_w[cl - 1:cl, :]
    to_end_w = jnp.exp(a_last_w - a_cum_w)
    from_start_w = jnp.exp(a_cum_w)
    chunk_decay_w = jnp.exp(a_last_w)
    xc = xs * dt_w
    xw = (xc * to_end_w).astype(BF16)
    xc_b = xc.astype(BF16)
    low_half = lane < SSM_HEAD_DIM

    y_parts = []
    for g in range(SSM_GROUPS):
        bg = b_mat[:, g * SSM_STATE:(g + 1) * SSM_STATE].astype(BF16)
        cg = c_mat[:, g * SSM_STATE:(g + 1) * SSM_STATE].astype(BF16)
        cb = lax.dot_general(cg, bg, (((1,), (1,)), ((), ())), preferred_element_type=F32)
        state = state_ref[g]
        y_off = jnp.dot(cg, state.astype(BF16), preferred_element_type=F32)
        y_off = y_off * from_start_w[:, g * gw:(g + 1) * gw]
        new_state = lax.dot_general(bg, xw[:, g * gw:(g + 1) * gw], (((0,), (0,)), ((), ())),
                                    preferred_element_type=F32)
        state_ref[g] = state * chunk_decay_w[:, g * gw:(g + 1) * gw] + new_state
        for pair in range(hg // 2):
            h0 = g * hg + 2 * pair
            c0 = h0 * SSM_HEAD_DIM
            x_pair = xc_b[:, c0:c0 + LANES]
            y_pair = None
            for k in range(2):
                h = h0 + k
                seg = a_cum[:, h:h + 1] - a_cum_t[h:h + 1, :]
                decay = jnp.where(causal, jnp.exp(jnp.where(causal, seg, 0.0)), 0.0)
                m = (cb * decay).astype(BF16)
                x_half = jnp.where(low_half if k == 0 else jnp.logical_not(low_half),
                                   x_pair, jnp.zeros_like(x_pair))
                part = jnp.dot(m, x_half, preferred_element_type=F32)
                y_pair = part if y_pair is None else y_pair + part
            y_parts.append(y_pair + y_off[:, c0 - g * gw:c0 - g * gw + LANES])
    y = jnp.concatenate(y_parts, axis=-1) + dsk_ref[...] * xs
    yz = y * (z * _sigmoid(z))
    out = yz * lax.rsqrt(jnp.mean(yz * yz, axis=-1, keepdims=True) + EPS) * gn_ref[...]
    o_ref[...] = out.astype(o_ref.dtype)


def _ssd(mid, offsets, conv_w, conv_b, dt_bias_pad, a_pad, d_skip_w, g_norm, expand):
    t, width = mid.shape
    cl = SSD_CHUNK
    off_z, off_xbc, off_dt = offsets

    def full(shape):
        return pl.BlockSpec(shape, lambda i: (0,) * len(shape))

    return pl.pallas_call(
        functools.partial(_ssd_body, cl=cl, off_z=off_z, off_xbc=off_xbc, off_dt=off_dt),
        grid=(t // cl,),
        in_specs=[pl.BlockSpec((cl, width), lambda i: (i, 0)),
                  full((CONV_WIDTH, SSM_CONV_DIM)), full((1, SSM_CONV_DIM)),
                  full((1, LANES)), full((1, LANES)), full((1, SSM_INNER)),
                  full((1, SSM_INNER)), full((LANES, SSM_INNER))],
        out_specs=pl.BlockSpec((cl, SSM_INNER), lambda i: (i, 0)),
        out_shape=jax.ShapeDtypeStruct((t, SSM_INNER), BF16),
        scratch_shapes=[pltpu.VMEM((cl + 8, SSM_CONV_DIM), F32),
                        pltpu.VMEM((SSM_GROUPS, SSM_STATE, SSM_INNER // SSM_GROUPS), F32)],
        compiler_params=_params(("arbitrary",)),
        name="ssd_mixer",
    )(mid, conv_w, conv_b, dt_bias_pad, a_pad, d_skip_w, g_norm, expand)


def _merge_body(mid_ref, g_ref, gt_ref, osb_ref, ofx_ref, osm_ref, wsb_ref, wfx_ref, wsm_ref, o_ref,
                *, head_cols, gate_off, tn):
    head = _sigmoid(mid_ref[:, IN_TILE - head_cols:])
    gates = jnp.concatenate([head, g_ref[...].astype(F32), gt_ref[...].astype(F32)], axis=1)
    branches = ((osb_ref[...], wsb_ref), (ofx_ref[...], wfx_ref), (osm_ref[...], wsm_ref))
    for c in range(D_MODEL // tn):
        merged = None
        for b, (act, w_ref) in enumerate(branches):
            start = gate_off + b * D_MODEL + c * tn
            term = gates[:, start:start + tn] * jnp.dot(act, w_ref[:, c * tn:(c + 1) * tn],
                                                        preferred_element_type=F32)
            merged = term if merged is None else merged + term
        o_ref[:, c * tn:(c + 1) * tn] = merged.astype(o_ref.dtype)


def _merge(mid, gates, gates_tail, head_cols, gate_off, o_sb, o_fox, o_ssm, w_sb, w_fox, w_ssm):
    t = mid.shape[0]
    tm = 256

    def rows(width, col_block=0):
        return pl.BlockSpec((tm, width), lambda i: (i, col_block))

    def whole(shape):
        return pl.BlockSpec(shape, lambda i: (0, 0))

    return pl.pallas_call(
        functools.partial(_merge_body, head_cols=head_cols, gate_off=gate_off, tn=512),
        grid=(t // tm,),
        in_specs=[rows(IN_TILE, mid.shape[1] // IN_TILE - 1), rows(gates.shape[1]),
                  rows(gates_tail.shape[1]),
                  rows(SB_WIDTH), rows(FOX_WIDTH), rows(SSM_INNER),
                  whole(w_sb.shape), whole(w_fox.shape), whole(w_ssm.shape)],
        out_specs=rows(D_MODEL),
        out_shape=jax.ShapeDtypeStruct((t, D_MODEL), BF16),
        compiler_params=_params(("arbitrary",)),
        name="branch_merge",
    )(mid, gates, gates_tail, o_sb, o_fox, o_ssm, w_sb, w_fox, w_ssm)


def _route(logits_t, b_router, count_ref, tm):
    aff = _sigmoid(logits_t)
    sel = aff + b_router
    rows = [sel[e:e + 1, :] for e in range(N_EXPERTS)]
    best_score, best_group = None, None
    for g in range(N_EXPERT_GROUPS):
        r = rows[g * EXPERTS_PER_GROUP:(g + 1) * EXPERTS_PER_GROUP]
        score = None
        for a in range(EXPERTS_PER_GROUP):
            for b in range(a + 1, EXPERTS_PER_GROUP):
                pair = r[a] + r[b]
                score = pair if score is None else jnp.maximum(score, pair)
        if g == 0:
            best_score, best_group = score, jnp.zeros_like(score, dtype=jnp.int32)
        else:
            better = score > best_score
            best_group = jnp.where(better, g, best_group)
            best_score = jnp.where(better, score, best_score)
    masked = [jnp.where(best_group == e // EXPERTS_PER_GROUP, rows[e], -jnp.inf)
              for e in range(N_EXPERTS)]

    def arg_top(vals):
        top, idx = vals[0], jnp.zeros_like(best_group)
        for e in range(1, N_EXPERTS):
            better = vals[e] > top
            idx = jnp.where(better, e, idx)
            top = jnp.where(better, vals[e], top)
        return idx

    e1 = arg_top(masked)
    e2 = arg_top([jnp.where(e1 == e, -jnp.inf, masked[e]) for e in range(N_EXPERTS)])
    expert_id = lax.broadcasted_iota(jnp.int32, (N_EXPERTS, tm), 0)
    pick1 = expert_id == e1
    pick2 = expert_id == e2
    a1 = jnp.sum(jnp.where(pick1, aff, 0.0), axis=0, keepdims=True)
    a2 = jnp.sum(jnp.where(pick2, aff, 0.0), axis=0, keepdims=True)
    denom = a1 + a2
    chosen = jnp.logical_or(pick1, pick2)
    t_row = lax.broadcasted_iota(jnp.int32, (tm, tm), 0)
    t_col = lax.broadcasted_iota(jnp.int32, (tm, tm), 1)
    earlier = (t_row < t_col).astype(BF16)
    rank = jnp.dot(chosen.astype(BF16), earlier, preferred_element_type=F32) + count_ref[...]
    r1 = jnp.sum(jnp.where(pick1, rank, 0.0), axis=0, keepdims=True)
    r2 = jnp.sum(jnp.where(pick2, rank, 0.0), axis=0, keepdims=True)
    count_ref[...] = count_ref[...] + jnp.sum(chosen.astype(F32), axis=1, keepdims=True)
    zero = jnp.zeros_like(a1)
    return jnp.concatenate([e1.astype(F32), e2.astype(F32), r1, r2, a1 / denom, a2 / denom,
                            zero, zero], axis=0)


def _outproj_body(m_ref, w_ref, x_ref, gm_ref, g_ref, sc_ref, sh_ref, wr_ref, br_ref,
                  xo_ref, hp_ref, rt_ref, cnt_ref, count_ref, *, tm):
    @pl.when(pl.program_id(0) == 0)
    def _():
        count_ref[...] = jnp.zeros_like(count_ref)

    x_new = x_ref[...] + gm_ref[...] * jnp.dot(m_ref[...], w_ref[...], preferred_element_type=F32)
    xo_ref[...] = x_new
    h = _modnorm(x_new, g_ref[...], sc_ref[...], sh_ref[...])
    hp_ref[...] = h
    logits_t = lax.dot_general(wr_ref[...], h, (((1,), (1,)), ((), ())),
                               precision=HIGHEST, preferred_element_type=F32)
    rt_ref[...] = _route(logits_t, br_ref[...], count_ref, tm)
    cnt_ref[...] = jnp.broadcast_to(count_ref[...], cnt_ref.shape)


def _outproj_route(merged, w_out, x, gate_m, g_ffn, scale_f, shift_f, w_router_t, b_router_col):
    t, d = x.shape
    tm = ROW_TILE
    vec = pl.BlockSpec((1, d), lambda i: (0, 0))
    return pl.pallas_call(
        functools.partial(_outproj_body, tm=tm),
        grid=(t // tm,),
        in_specs=[pl.BlockSpec((tm, d), lambda i: (i, 0)),
                  pl.BlockSpec((d, d), lambda i: (0, 0)),
                  pl.BlockSpec((tm, d), lambda i: (i, 0)),
                  vec, vec, vec, vec,
                  pl.BlockSpec((N_EXPERTS, d), lambda i: (0, 0)),
                  pl.BlockSpec((N_EXPERTS, 1), lambda i: (0, 0))],
        out_specs=[pl.BlockSpec((tm, d), lambda i: (i, 0)),
                   pl.BlockSpec((tm, d), lambda i: (i, 0)),
                   pl.BlockSpec((8, tm), lambda i: (0, i)),
                   pl.BlockSpec((N_EXPERTS, LANES), lambda i: (0, 0))],
        out_shape=[jax.ShapeDtypeStruct((t, d), F32),
                   jax.ShapeDtypeStruct((t, d), F32),
                   jax.ShapeDtypeStruct((8, t), F32),
                   jax.ShapeDtypeStruct((N_EXPERTS, LANES), F32)],
        scratch_shapes=[pltpu.VMEM((N_EXPERTS, 1), F32)],
        compiler_params=_params(("arbitrary",)),
        name="outproj_route",
    )(merged, w_out, x, gate_m, g_ffn, scale_f, shift_f, w_router_t, b_router_col)


def _dispatch_body(dest_ref, h_ref, init_ref, xs_ref, sem, *, tm, t):
    del init_ref
    base = pl.program_id(0) * tm

    def row_copy(r, d):
        return pltpu.make_async_copy(h_ref.at[pl.ds(r, 1)], xs_ref.at[pl.ds(d, 1)], sem)

    def issue(r, carry):
        row_copy(r, dest_ref[base + r]).start()
        row_copy(r, dest_ref[t + base + r]).start()
        return carry

    lax.fori_loop(0, tm, issue, 0)

    def drain(r, carry):
        row_copy(r, 0).wait()
        row_copy(r, 0).wait()
        return carry

    lax.fori_loop(0, tm, drain, 0)


def _dispatch(dest, h, n_rows):
    t, d = h.shape
    tm = ROW_TILE
    init = jnp.zeros((n_rows, d), F32)
    return pl.pallas_call(
        functools.partial(_dispatch_body, tm=tm, t=t),
        grid_spec=pltpu.PrefetchScalarGridSpec(
            num_scalar_prefetch=1,
            grid=(t // tm,),
            in_specs=[pl.BlockSpec((tm, d), lambda i, dest: (i, 0)),
                      pl.BlockSpec(memory_space=pl.ANY)],
            out_specs=pl.BlockSpec(memory_space=pl.ANY),
            scratch_shapes=[pltpu.SemaphoreType.DMA(())]),
        out_shape=jax.ShapeDtypeStruct((n_rows, d), F32),
        input_output_aliases={2: 0},
        compiler_params=_params(("arbitrary",)),
        name="moe_dispatch",
    )(dest, h, init)


def _expert_body(te_ref, nu_ref, xs_ref, wg_ref, wu_ref, wd_ref, o_ref):
    del te_ref

    @pl.when(pl.program_id(0) < nu_ref[0])
    def _():
        xb = xs_ref[...].astype(BF16)
        gate = jnp.dot(xb, wg_ref[...], preferred_element_type=F32)
        up = jnp.dot(xb, wu_ref[...], preferred_element_type=F32)
        act = (gate * _sigmoid(gate) * up).astype(BF16)
        o_ref[...] = jnp.dot(act, wd_ref[...], preferred_element_type=F32)

    @pl.when(pl.program_id(0) >= nu_ref[0])
    def _():
        o_ref[...] = jnp.zeros_like(o_ref)


def _experts(tile_expert, n_used, xs, layer, w_gate, w_up, w_down):
    n_rows = xs.shape[0]
    te = EXPERT_TILE

    def row_map(i, tex, nu):
        return (jnp.maximum(jnp.minimum(i, nu[0] - 1), 0), 0)

    def w_map(i, tex, nu):
        return (layer, tex[i], 0, 0)

    return pl.pallas_call(
        _expert_body,
        grid_spec=pltpu.PrefetchScalarGridSpec(
            num_scalar_prefetch=2,
            grid=(n_rows // te,),
            in_specs=[pl.BlockSpec((te, D_MODEL), row_map),
                      pl.BlockSpec((None, None, D_MODEL, D_FF_EXPERT), w_map),
                      pl.BlockSpec((None, None, D_MODEL, D_FF_EXPERT), w_map),
                      pl.BlockSpec((None, None, D_FF_EXPERT, D_MODEL), w_map)],
            out_specs=pl.BlockSpec((te, D_MODEL), lambda i, tex, nu: (i, 0))),
        out_shape=jax.ShapeDtypeStruct((n_rows, D_MODEL), F32),
        compiler_params=_params(("arbitrary",)),
        name="moe_experts",
    )(tile_expert, n_used, xs, w_gate, w_up, w_down)


def _combine_body(dest_ref, ys_ref, x_ref, w_ref, gf_ref, o_ref, a_buf, b_buf, sem, *, tm, t):
    base = pl.program_id(0) * tm

    def row_copy(d, buf, r):
        return pltpu.make_async_copy(ys_ref.at[pl.ds(d, 1)], buf.at[pl.ds(r, 1)], sem)

    def issue(r, carry):
        row_copy(dest_ref[base + r], a_buf, r).start()
        row_copy(dest_ref[t + base + r], b_buf, r).start()
        return carry

    lax.fori_loop(0, tm, issue, 0)

    def drain(r, carry):
        row_copy(0, a_buf, r).wait()
        row_copy(0, b_buf, r).wait()
        return carry

    lax.fori_loop(0, tm, drain, 0)
    w = w_ref[...]
    y = w[:, 0:1] * a_buf[...] + w[:, 1:2] * b_buf[...]
    o_ref[...] = x_ref[...] + gf_ref[...] * y


def _combine(dest, ys, x, w_cols, gate_f):
    t, d = x.shape
    tm = COMBINE_TILE
    return pl.pallas_call(
        functools.partial(_combine_body, tm=tm, t=t),
        grid_spec=pltpu.PrefetchScalarGridSpec(
            num_scalar_prefetch=1,
            grid=(t // tm,),
            in_specs=[pl.BlockSpec(memory_space=pl.ANY),
                      pl.BlockSpec((tm, d), lambda i, dest: (i, 0)),
                      pl.BlockSpec((tm, 8), lambda i, dest: (i, 0)),
                      pl.BlockSpec((1, d), lambda i, dest: (0, 0))],
            out_specs=pl.BlockSpec((tm, d), lambda i, dest: (i, 0)),
            scratch_shapes=[pltpu.VMEM((tm, d), F32), pltpu.VMEM((tm, d), F32),
                            pltpu.SemaphoreType.DMA(())]),
        out_shape=jax.ShapeDtypeStruct((t, d), F32),
        compiler_params=_params(("arbitrary",)),
        name="moe_combine",
    )(dest, ys, x, w_cols, gate_f)


def _pad_lanes(v, fill=0.0):
    return jnp.pad(v.astype(F32), (0, LANES - v.shape[0]), constant_values=fill).reshape(1, LANES)


def _routing_tables(route, counts, t):
    te = EXPERT_TILE
    n_tiles = (2 * t) // te + N_EXPERTS
    cnt = counts[:, 0].astype(jnp.int32)
    padded = ((cnt + te - 1) // te) * te
    ends = jnp.cumsum(padded)
    offsets = ends - padded
    e1 = route[0].astype(jnp.int32)
    e2 = route[1].astype(jnp.int32)
    d1 = offsets[e1] + route[2].astype(jnp.int32)
    d2 = offsets[e2] + route[3].astype(jnp.int32)
    dest = jnp.concatenate([d1, d2]).astype(jnp.int32)
    n_used = (ends[-1] // te).astype(jnp.int32)
    tile_start = jnp.arange(n_tiles, dtype=jnp.int32) * te
    tile_expert = jnp.sum((tile_start[:, None] >= ends[None, :]).astype(jnp.int32), axis=1)
    last_expert = tile_expert[jnp.maximum(n_used - 1, 0)]
    tile_expert = jnp.where(jnp.arange(n_tiles) < n_used, tile_expert, last_expert)
    tile_expert = jnp.minimum(tile_expert, N_EXPERTS - 1).astype(jnp.int32)
    w_cols = jnp.transpose(route)
    w_cols = jnp.concatenate([w_cols[:, 4:6], w_cols[:, 0:6]], axis=1)
    return dest, tile_expert, n_used.reshape(1), w_cols, n_tiles * te


def kernel(x, c, w_ada, b_ada, g_norm_mix, w_in, b_fgate, g_q_fox, g_k_fox, conv_w, conv_b,
           dt_bias, a_log, d_skip, g_ssm_norm, w_branch_sb, w_branch_fox, w_branch_ssm, w_out,
           g_norm_ffn, w_router, b_router, w_e_gate, w_e_up, w_e_down):
    bsz, t, d = x.shape
    assert bsz == 1 and d == D_MODEL
    n_layers = w_ada.shape[0]
    xt = x.reshape(t, d)

    mod = _ada(c, w_ada, b_ada)
    expand = (jnp.arange(SSM_INNER)[None, :] // SSM_HEAD_DIM
              == jnp.arange(LANES)[:, None]).astype(F32)
    w_router_t = jnp.transpose(w_router)
    b_router_col = b_router.reshape(N_EXPERTS, 1)

    o_ff = 3 * SB_WIDTH + 3 * FOX_WIDTH
    o_z = o_ff + FOX_HEADS
    o_xbc = o_z + SSM_INNER
    o_dt = o_xbc + SSM_CONV_DIM
    o_gate = o_dt + SSM_HEADS
    width = w_in.shape[2]
    assert o_ff % IN_TILE == 0 and o_gate + N_BRANCH * d == width
    qkv_tiles = o_ff // IN_TILE
    mid_end = -(-o_gate // IN_TILE) * IN_TILE
    mid_tiles = (mid_end - o_ff) // IN_TILE
    full_end = width // IN_TILE * IN_TILE
    head_start = o_gate // LANES * LANES
    tail_pad = -(-(width - full_end) // LANES) * LANES
    q_scale = jnp.concatenate([jnp.full((SB_WIDTH,), HEAD_DIM ** -0.5 * LOG2E, F32),
                               jnp.ones((o_ff - SB_WIDTH,), F32)]).reshape(1, o_ff)
    w_gate_b, w_up_b, w_down_b = _cast_bf16(w_e_gate), _cast_bf16(w_e_up), _cast_bf16(w_e_down)

    for layer in range(n_layers):
        m = mod[layer].reshape(6, 1, d)
        shift_m, scale_m, gate_m, shift_f, scale_f, gate_f = (m[i] for i in range(6))
        w_tail = jnp.pad(w_in[layer][:, full_end:],
                         ((0, 0), (0, tail_pad - (width - full_end)))).astype(BF16)

        h = _norm(xt, g_norm_mix[layer].reshape(1, d), scale_m, shift_m)
        qkv = _proj(h, w_in, layer, 0, qkv_tiles, BF16, col_scale=q_scale, name="in_proj_qkv")
        mid = _proj(h, w_in, layer, qkv_tiles, mid_tiles, F32, name="in_proj_mid")
        gates = _proj(h, w_in, layer, mid_end // IN_TILE, (full_end - mid_end) // IN_TILE, BF16,
                      act="sigmoid", name="in_proj_gates")
        gates_tail = _mm(h, w_tail, BF16, tail_pad, act="sigmoid", name="in_proj_gates_tail")

        o_sb = _sb_attention(qkv)
        qn, kn, f_col = _fox_prep(qkv, mid, _pad_lanes(b_fgate[layer]),
                                  g_q_fox[layer].reshape(1, HEAD_DIM),
                                  g_k_fox[layer].reshape(1, HEAD_DIM))
        f_row = jnp.transpose(f_col[:, :8])
        o_fox = _fox_attention(qn, kn, qkv, f_col, f_row)
        o_ssm = _ssd(mid, (o_z - o_ff, o_xbc - o_ff, o_dt - o_ff), conv_w[layer],
                     conv_b[layer].reshape(1, SSM_CONV_DIM),
                     _pad_lanes(dt_bias[layer]), _pad_lanes(-jnp.exp(a_log[layer].astype(F32))),
                     jnp.repeat(d_skip[layer].astype(F32), SSM_HEAD_DIM).reshape(1, SSM_INNER),
                     g_ssm_norm[layer].reshape(1, SSM_INNER), expand)

        merged = _merge(mid, gates, gates_tail, mid_end - head_start, o_gate - head_start,
                        o_sb, o_fox, o_ssm, w_branch_sb[layer].astype(BF16),
                        w_branch_fox[layer].astype(BF16), w_branch_ssm[layer].astype(BF16))
        xt, h_moe, route, counts = _outproj_route(
            merged, w_out[layer].astype(BF16), xt, gate_m, g_norm_ffn[layer].reshape(1, d),
            scale_f, shift_f, w_router_t, b_router_col)

        dest, tile_expert, n_used, w_cols, n_rows = _routing_tables(route, counts, t)
        xs = _dispatch(dest, h_moe, n_rows)
        ys = _experts(tile_expert, n_used, xs, layer, w_gate_b, w_up_b, w_down_b)
        xt = _combine(dest, ys, xt, w_cols, gate_f)
    return xt.reshape(bsz, t, d)
```

```python
import functools

import jax
import jax.numpy as jnp
from jax import lax
from jax.experimental import pallas as pl
from jax.experimental.pallas import tpu as pltpu

F32 = jnp.float32
BF16 = jnp.bfloat16
HIGHEST = lax.Precision.HIGHEST
LOG2E = 1.4426950408889634

D_MODEL = 2048
EPS = 1e-6
HEAD_DIM = 128
SB_HEADS = 4
FOX_HEADS = 4
SB_WIDTH = SB_HEADS * HEAD_DIM
FOX_WIDTH = FOX_HEADS * HEAD_DIM
SSM_HEAD_DIM = 64
SSM_HEADS = 16
SSM_INNER = SSM_HEADS * SSM_HEAD_DIM
SSM_GROUPS = 2
SSM_STATE = 128
CONV_WIDTH = 4
SSM_CONV_DIM = SSM_INNER + 2 * SSM_GROUPS * SSM_STATE
N_BRANCH = 3
N_EXPERTS = 16
N_EXPERT_GROUPS = 4
EXPERTS_PER_GROUP = N_EXPERTS // N_EXPERT_GROUPS
D_FF_EXPERT = 1024

LANES = 128
V7X_VMEM_BYTES = 64 * 1024 * 1024
VMEM_LIMIT = 48 * 1024 * 1024

ROW_TILE = 512
IN_TILE = 1024
ATTN_TILE = 256
ATTN_SUB = 4
SSD_CHUNK = 128
EXPERT_TILE = 256
COMBINE_TILE = 256


def _params(sem, vmem=VMEM_LIMIT):
    return pltpu.CompilerParams(dimension_semantics=sem, vmem_limit_bytes=vmem)


def _sigmoid(x):
    return 1.0 / (1.0 + jnp.exp(-x))


def _log_sigmoid(x):
    return jnp.minimum(x, 0.0) - jnp.log(1.0 + jnp.exp(-jnp.abs(x)))


def _softplus(x):
    return jnp.maximum(x, 0.0) + jnp.log(1.0 + jnp.exp(-jnp.abs(x)))


def _ada_body(c_ref, w_ref, b_ref, o_ref):
    k = pl.program_id(1)

    @pl.when(k == 0)
    def _():
        o_ref[0] = b_ref[0]

    c = c_ref[...]
    cond = c * _sigmoid(c)
    parts = [jnp.sum(w_ref[0, :, j * LANES:(j + 1) * LANES] * cond, axis=0, keepdims=True)
             for j in range(w_ref.shape[2] // LANES)]
    o_ref[0] += jnp.concatenate(parts, axis=1)


def _ada(c, w_ada, b_ada):
    n_layers, d, n = w_ada.shape
    tk = 256
    c_lanes = jnp.broadcast_to(c.reshape(d, 1), (d, LANES))
    out = pl.pallas_call(
        _ada_body,
        grid=(n_layers, d // tk),
        in_specs=[pl.BlockSpec((tk, LANES), lambda l, k: (k, 0)),
                  pl.BlockSpec((1, tk, n), lambda l, k: (l, k, 0)),
                  pl.BlockSpec((1, 1, n), lambda l, k: (l, 0, 0))],
        out_specs=pl.BlockSpec((1, 1, n), lambda l, k: (l, 0, 0)),
        out_shape=jax.ShapeDtypeStruct((n_layers, 1, n), F32),
        compiler_params=_params(("arbitrary", "arbitrary")),
        name="ada_mod",
    )(c_lanes, w_ada, b_ada.reshape(n_layers, 1, n))
    return out[:, 0, :]


def _modnorm(x, g, scale, shift):
    y = x * lax.rsqrt(jnp.mean(x * x, axis=-1, keepdims=True) + EPS) * g
    return y * (1.0 + scale) + shift


def _norm_body(x_ref, g_ref, sc_ref, sh_ref, o_ref):
    o_ref[...] = _modnorm(x_ref[...], g_ref[...], sc_ref[...], sh_ref[...]).astype(BF16)


def _norm(x, g, scale, shift):
    t, d = x.shape
    vec = pl.BlockSpec((1, d), lambda i: (0, 0))
    return pl.pallas_call(
        _norm_body,
        grid=(t // ROW_TILE,),
        in_specs=[pl.BlockSpec((ROW_TILE, d), lambda i: (i, 0)), vec, vec, vec],
        out_specs=pl.BlockSpec((ROW_TILE, d), lambda i: (i, 0)),
        out_shape=jax.ShapeDtypeStruct((t, d), BF16),
        compiler_params=_params(("arbitrary",)),
        name="mod_norm",
    )(x, g, scale, shift)


def _mm_body(a_ref, w_ref, o_ref, *, act):
    r = jnp.dot(a_ref[...], w_ref[...].astype(BF16), preferred_element_type=F32)
    if act == "sigmoid":
        r = _sigmoid(r)
    o_ref[...] = r.astype(o_ref.dtype)


def _mm(a, w, out_dtype, tn, act=None, tm=1024, name="matmul"):
    m, k = a.shape
    n = w.shape[1]
    tm = min(tm, m)
    return pl.pallas_call(
        functools.partial(_mm_body, act=act),
        grid=(m // tm, n // tn),
        in_specs=[pl.BlockSpec((tm, k), lambda i, j: (i, 0)),
                  pl.BlockSpec((k, tn), lambda i, j: (0, j))],
        out_specs=pl.BlockSpec((tm, tn), lambda i, j: (i, j)),
        out_shape=jax.ShapeDtypeStruct((m, n), out_dtype),
        compiler_params=_params(("arbitrary", "arbitrary")),
        name=name,
    )(a, w)


def _cast_body(w_ref, o_ref):
    o_ref[...] = w_ref[...].astype(o_ref.dtype)


def _cast_bf16(w):
    shape = w.shape
    n = shape[-1]
    rows = w.size // n
    tr = min(rows, (2 * 1024 * 1024) // n)
    out = pl.pallas_call(
        _cast_body,
        grid=(rows // tr,),
        in_specs=[pl.BlockSpec((tr, n), lambda i: (i, 0))],
        out_specs=pl.BlockSpec((tr, n), lambda i: (i, 0)),
        out_shape=jax.ShapeDtypeStruct((rows, n), BF16),
        compiler_params=_params(("arbitrary",)),
        name="cast_bf16",
    )(w.reshape(rows, n))
    return out.reshape(shape)


def _proj_body(*refs, act, scaled):
    if scaled:
        h_ref, w_ref, sc_ref, o_ref, wb_ref = refs
    else:
        h_ref, w_ref, o_ref, wb_ref = refs

    @pl.when(pl.program_id(1) == 0)
    def _():
        wb_ref[...] = w_ref[0].astype(BF16)

    r = jnp.dot(h_ref[...], wb_ref[...], preferred_element_type=F32)
    if scaled:
        r = r * sc_ref[...]
    if act == "sigmoid":
        r = _sigmoid(r)
    o_ref[...] = r.astype(o_ref.dtype)


def _proj(h, w_in, layer, first_tile, n_tiles, out_dtype, act=None, col_scale=None, name="in_proj"):
    t, k = h.shape
    tm, tn = 1024, IN_TILE
    assert (first_tile + n_tiles) * tn <= w_in.shape[2]
    in_specs = [pl.BlockSpec((tm, k), lambda j, i: (i, 0)),
                pl.BlockSpec((1, k, tn), lambda j, i: (layer, 0, first_tile + j))]
    args = [h, w_in]
    if col_scale is not None:
        in_specs.append(pl.BlockSpec((1, tn), lambda j, i: (0, j)))
        args.append(col_scale)
    return pl.pallas_call(
        functools.partial(_proj_body, act=act, scaled=col_scale is not None),
        grid=(n_tiles, t // tm),
        in_specs=in_specs,
        out_specs=pl.BlockSpec((tm, tn), lambda j, i: (i, j)),
        out_shape=jax.ShapeDtypeStruct((t, n_tiles * tn), out_dtype),
        scratch_shapes=[pltpu.VMEM((k, tn), BF16)],
        compiler_params=_params(("arbitrary", "arbitrary")),
        name=name,
    )(*args)


def _fox_prep_body(q_ref, k_ref, f_ref, bf_ref, gq_ref, gk_ref,
                   qo_ref, ko_ref, fo_ref, carry_ref, *, tm):
    @pl.when(pl.program_id(0) == 0)
    def _():
        carry_ref[...] = jnp.zeros_like(carry_ref)

    scale = HEAD_DIM ** -0.5 * LOG2E
    for h in range(FOX_HEADS):
        sl = slice(h * HEAD_DIM, (h + 1) * HEAD_DIM)
        q = q_ref[:, sl].astype(F32)
        qn = q * lax.rsqrt(jnp.mean(q * q, axis=-1, keepdims=True) + EPS) * gq_ref[...]
        qo_ref[:, sl] = (qn * scale).astype(BF16)
        k = k_ref[:, sl].astype(F32)
        kn = k * lax.rsqrt(jnp.mean(k * k, axis=-1, keepdims=True) + EPS) * gk_ref[...]
        ko_ref[:, sl] = kn.astype(BF16)

    log_f = _log_sigmoid(f_ref[...] + bf_ref[...])
    row = lax.broadcasted_iota(jnp.int32, (tm, tm), 0)
    col = lax.broadcasted_iota(jnp.int32, (tm, tm), 1)
    tri = (col <= row).astype(F32)
    cum = jnp.dot(tri, log_f, precision=HIGHEST, preferred_element_type=F32) + carry_ref[...]
    fo_ref[...] = cum * LOG2E
    carry_ref[...] = cum[tm - 1:tm, :]


def _fox_prep(qkv, misc, b_fgate_pad, g_q, g_k):
    t = qkv.shape[0]
    tm = ROW_TILE
    q_blk = 3 * SB_WIDTH // FOX_WIDTH
    vec = pl.BlockSpec((1, LANES), lambda i: (0, 0))
    return pl.pallas_call(
        functools.partial(_fox_prep_body, tm=tm),
        grid=(t // tm,),
        in_specs=[pl.BlockSpec((tm, FOX_WIDTH), lambda i: (i, q_blk)),
                  pl.BlockSpec((tm, FOX_WIDTH), lambda i: (i, q_blk + 1)),
                  pl.BlockSpec((tm, LANES), lambda i: (i, 0)),
                  vec, vec, vec],
        out_specs=[pl.BlockSpec((tm, FOX_WIDTH), lambda i: (i, 0)),
                   pl.BlockSpec((tm, FOX_WIDTH), lambda i: (i, 0)),
                   pl.BlockSpec((tm, LANES), lambda i: (i, 0))],
        out_shape=[jax.ShapeDtypeStruct((t, FOX_WIDTH), BF16),
                   jax.ShapeDtypeStruct((t, FOX_WIDTH), BF16),
                   jax.ShapeDtypeStruct((t, LANES), F32)],
        scratch_shapes=[pltpu.VMEM((1, LANES), F32)],
        compiler_params=_params(("arbitrary",)),
        name="fox_prep",
    )(qkv, qkv, misc, b_fgate_pad, g_q, g_k)


def _sb_body(q_ref, k_ref, v_ref, o_ref, *, tb, nsub):
    qb = pl.program_id(1)
    qs = [q_ref[a * tb:(a + 1) * tb, :] for a in range(nsub)]
    row = lax.broadcasted_iota(jnp.int32, (tb, tb), 0)
    col = lax.broadcasted_iota(jnp.int32, (tb, tb), 1)
    ones = (row >= col).astype(BF16)
    suffix_ones = jnp.concatenate([ones, ones], axis=0)
    past = col < row

    def load(kb):
        ks = pl.multiple_of(kb * tb, tb)
        return k_ref[pl.ds(ks, tb), :], v_ref[pl.ds(ks, tb), :]

    def sweep(k, v, chains):
        us = [lax.dot_general(q, k, (((1,), (1,)), ((), ())), preferred_element_type=F32)
              for q, _, _, _ in chains]
        splits = []
        for u, (_, _, _, diagonal) in zip(us, chains):
            drop = jnp.maximum(u, 0.0) + jnp.log2(1.0 + jnp.exp2(-jnp.abs(u)))
            if diagonal:
                drop = jnp.where(past, drop, 0.0)
            hi = drop.astype(BF16)
            lo = (drop - hi.astype(F32)).astype(BF16)
            splits.append(jnp.concatenate([hi, lo], axis=1))
        sufs = [jnp.dot(s, suffix_ones, preferred_element_type=F32) for s in splits]
        ws = []
        for u, suf, (_, carry, _, diagonal) in zip(us, sufs, chains):
            w = jnp.exp2(u - suf - carry)
            if diagonal:
                w = jnp.where(past, w, 0.0)
            ws.append(w.astype(BF16))
        return [(carry + suf[:, 0:1], acc + jnp.dot(w, v, preferred_element_type=F32))
                for w, suf, (_, carry, acc, _) in zip(ws, sufs, chains)]

    state = [(jnp.zeros((tb, 1), F32), jnp.zeros((tb, HEAD_DIM), F32)) for _ in range(nsub)]
    for j in reversed(range(nsub)):
        k, v = load(qb * nsub + j)
        new = sweep(k, v, [(qs[a], state[a][0], state[a][1], a == j) for a in range(j, nsub)])
        state[j:] = new

    def step(i, c):
        k, v = load(qb * nsub - 1 - i)
        return tuple(sweep(k, v, [(qs[a], c[a][0], c[a][1], False) for a in range(nsub)]))

    state = lax.fori_loop(0, qb * nsub, step, tuple(state))
    for a in range(nsub):
        o_ref[a * tb:(a + 1) * tb, :] = state[a][1].astype(o_ref.dtype)


def _sb_attention(qkv):
    t = qkv.shape[0]
    tb, nsub = ATTN_TILE, ATTN_SUB
    return pl.pallas_call(
        functools.partial(_sb_body, tb=tb, nsub=nsub),
        grid=(SB_HEADS, t // (tb * nsub)),
        in_specs=[pl.BlockSpec((tb * nsub, HEAD_DIM), lambda h, i: (i, h)),
                  pl.BlockSpec((t, HEAD_DIM), lambda h, i: (0, SB_HEADS + h)),
                  pl.BlockSpec((t, HEAD_DIM), lambda h, i: (0, 2 * SB_HEADS + h))],
        out_specs=pl.BlockSpec((tb * nsub, HEAD_DIM), lambda h, i: (i, h)),
        out_shape=jax.ShapeDtypeStruct((t, SB_WIDTH), BF16),
        compiler_params=_params(("arbitrary", "arbitrary")),
        name="sb_attention",
    )(qkv, qkv, qkv)


def _fox_body(q_ref, k_ref, v_ref, fcol_ref, frow_ref, o_ref, *, tb, nsub):
    h = pl.program_id(0)
    qb = pl.program_id(1)
    lane = lax.broadcasted_iota(jnp.int32, (tb, LANES), 1)
    qs, f_qs = [], []
    for a in range(nsub):
        qs.append(q_ref[a * tb:(a + 1) * tb, :])
        f_qs.append(jnp.sum(jnp.where(lane == h, fcol_ref[a * tb:(a + 1) * tb, :], 0.0),
                            axis=-1, keepdims=True))
    row = lax.broadcasted_iota(jnp.int32, (tb, tb), 0)
    col = lax.broadcasted_iota(jnp.int32, (tb, tb), 1)
    causal = col <= row

    def load(kb):
        ks = pl.multiple_of(kb * tb, tb)
        return (k_ref[pl.ds(ks, tb), :], v_ref[pl.ds(ks, tb), :],
                frow_ref[pl.ds(h, 1), pl.ds(ks, tb)])

    def sweep(k, v, f_k, chains):
        ss = []
        for a, _, diagonal in chains:
            s = lax.dot_general(qs[a], k, (((1,), (1,)), ((), ())), preferred_element_type=F32) - f_k
            ss.append(jnp.where(causal, s, -jnp.inf) if diagonal else s)
        ms, ps, sums = [], [], []
        for s, (a, (m, _, _), _) in zip(ss, chains):
            m_new = jnp.maximum(m, jnp.max(s, axis=-1, keepdims=True) + f_qs[a])
            p = jnp.exp2(s - (m_new - f_qs[a]))
            ms.append(m_new)
            sums.append(jnp.sum(p, axis=-1, keepdims=True))
            ps.append(p.astype(BF16))
        out = []
        for p, p_sum, m_new, (_, (m, l, acc), _) in zip(ps, sums, ms, chains):
            alpha = jnp.exp2(m - m_new)
            out.append((m_new, alpha * l + p_sum,
                        alpha * acc + jnp.dot(p, v, preferred_element_type=F32)))
        return out

    def step(kb, c):
        k, v, f_k = load(kb)
        return tuple(sweep(k, v, f_k, [(a, c[a], False) for a in range(nsub)]))

    init = tuple((jnp.full((tb, 1), -jnp.inf, F32), jnp.zeros((tb, 1), F32),
                  jnp.zeros((tb, HEAD_DIM), F32)) for _ in range(nsub))
    state = list(lax.fori_loop(0, qb * nsub, step, init))
    for j in range(nsub):
        k, v, f_k = load(qb * nsub + j)
        state[j:] = sweep(k, v, f_k, [(a, state[a], a == j) for a in range(j, nsub)])
    for a in range(nsub):
        o_ref[a * tb:(a + 1) * tb, :] = (state[a][2] / state[a][1]).astype(o_ref.dtype)


def _fox_attention(qn, kn, qkv, f_col, f_row):
    t = qn.shape[0]
    tb, nsub = ATTN_TILE, ATTN_SUB
    v_blk = (3 * SB_WIDTH + 2 * FOX_WIDTH) // HEAD_DIM
    return pl.pallas_call(
        functools.partial(_fox_body, tb=tb, nsub=nsub),
        grid=(FOX_HEADS, t // (tb * nsub)),
        in_specs=[pl.BlockSpec((tb * nsub, HEAD_DIM), lambda h, i: (i, h)),
                  pl.BlockSpec((t, HEAD_DIM), lambda h, i: (0, h)),
                  pl.BlockSpec((t, HEAD_DIM), lambda h, i: (0, v_blk + h)),
                  pl.BlockSpec((tb * nsub, LANES), lambda h, i: (i, 0)),
                  pl.BlockSpec((8, t), lambda h, i: (0, 0))],
        out_specs=pl.BlockSpec((tb * nsub, HEAD_DIM), lambda h, i: (i, h)),
        out_shape=jax.ShapeDtypeStruct((t, FOX_WIDTH), BF16),
        compiler_params=_params(("arbitrary", "arbitrary")),
        name="fox_attention",
    )(qn, kn, qkv, f_col, f_row)


def _ssd_body(mid_ref, cw_ref, cb_ref, dtb_ref, a_ref, dsk_ref, gn_ref, ex_ref,
              o_ref, ext_ref, state_ref, *, cl, off_z, off_xbc, off_dt):
    hg = SSM_HEADS // SSM_GROUPS
    gw = hg * SSM_HEAD_DIM

    @pl.when(pl.program_id(0) == 0)
    def _():
        ext_ref[0:8, :] = jnp.zeros((8, SSM_CONV_DIM), F32)
        state_ref[...] = jnp.zeros_like(state_ref)

    z = mid_ref[:, off_z:off_z + SSM_INNER]
    xbc_raw = mid_ref[:, off_xbc:off_xbc + SSM_CONV_DIM]
    lane = lax.broadcasted_iota(jnp.int32, (cl, LANES), 1)
    dt_raw = jnp.where(lane < SSM_HEADS, mid_ref[:, off_dt:off_dt + LANES], 0.0)

    ext_ref[8:8 + cl, :] = xbc_raw
    conv = cb_ref[...] + cw_ref[0:1, :] * ext_ref[pl.ds(8 - (CONV_WIDTH - 1), cl), :]
    for w in range(1, CONV_WIDTH):
        conv = conv + cw_ref[w:w + 1, :] * ext_ref[pl.ds(8 - (CONV_WIDTH - 1) + w, cl), :]
    ext_ref[0:8, :] = xbc_raw[cl - 8:cl, :]
    xbc = conv * _sigmoid(conv)
    xs = xbc[:, :SSM_INNER]
    b_mat = xbc[:, SSM_INNER:SSM_INNER + SSM_GROUPS * SSM_STATE]
    c_mat = xbc[:, SSM_INNER + SSM_GROUPS * SSM_STATE:]

    dt = _softplus(dt_raw + dtb_ref[...])
    log_a = dt * a_ref[...]
    row = lax.broadcasted_iota(jnp.int32, (cl, cl), 0)
    col = lax.broadcasted_iota(jnp.int32, (cl, cl), 1)
    causal = col <= row
    a_cum = jnp.dot(causal.astype(F32), log_a, precision=HIGHEST, preferred_element_type=F32)
    a_cum_t = a_cum.T
    expand = ex_ref[...]

    def widen(v):
        return jnp.dot(v, expand, precision=HIGHEST, preferred_element_type=F32)

    dt_w = widen(dt)
    a_cum_w = widen(a_cum)
    a_last_w = a_cum_w[cl - 1:cl, :]
    to_end_w = jnp.exp(a_last_w - a_cum_w)
    from_start_w = jnp.exp(a_cum_w)
    chunk_decay_w = jnp.exp(a_last_w)
    xc = xs * dt_w
    xw = (xc * to_end_w).astype(BF16)
    xc_b = xc.astype(BF16)
    low_half = lane < SSM_HEAD_DIM

    y_parts = []
    for g in range(SSM_GROUPS):
        bg = b_mat[:, g * SSM_STATE:(g + 1) * SSM_STATE].astype(BF16)
        cg = c_mat[:, g * SSM_STATE:(g + 1) * SSM_STATE].astype(BF16)
        cb = lax.dot_general(cg, bg, (((1,), (1,)), ((), ())), preferred_element_type=F32)
        state = state_ref[g]
        y_off = jnp.dot(cg, state.astype(BF16), preferred_element_type=F32)
        y_off = y_off * from_start_w[:, g * gw:(g + 1) * gw]
        new_state = lax.dot_general(bg, xw[:, g * gw:(g + 1) * gw], (((0,), (0,)), ((), ())),
                                    preferred_element_type=F32)
        state_ref[g] = state * chunk_decay_w[:, g * gw:(g + 1) * gw] + new_state
        for pair in range(hg // 2):
            h0 = g * hg + 2 * pair
            c0 = h0 * SSM_HEAD_DIM
            x_pair = xc_b[:, c0:c0 + LANES]
            y_pair = None
            for k in range(2):
                h = h0 + k
                seg = a_cum[:, h:h + 1] - a_cum_t[h:h + 1, :]
                decay = jnp.where(causal, jnp.exp(jnp.where(causal, seg, 0.0)), 0.0)
                m = (cb * decay).astype(BF16)
                x_half = jnp.where(low_half if k == 0 else jnp.logical_not(low_half),
                                   x_pair, jnp.zeros_like(x_pair))
                part = jnp.dot(m, x_half, preferred_element_type=F32)
                y_pair = part if y_pair is None else y_pair + part
            y_parts.append(y_pair + y_off[:, c0 - g * gw:c0 - g * gw + LANES])
    y = jnp.concatenate(y_parts, axis=-1) + dsk_ref[...] * xs
    yz = y * (z * _sigmoid(z))
    out = yz * lax.rsqrt(jnp.mean(yz * yz, axis=-1, keepdims=True) + EPS) * gn_ref[...]
    o_ref[...] = out.astype(o_ref.dtype)


def _ssd(mid, offsets, conv_w, conv_b, dt_bias_pad, a_pad, d_skip_w, g_norm, expand):
    t, width = mid.shape
    cl = SSD_CHUNK
    off_z, off_xbc, off_dt = offsets

    def full(shape):
        return pl.BlockSpec(shape, lambda i: (0,) * len(shape))

    return pl.pallas_call(
        functools.partial(_ssd_body, cl=cl, off_z=off_z, off_xbc=off_xbc, off_dt=off_dt),
        grid=(t // cl,),
        in_specs=[pl.BlockSpec((cl, width), lambda i: (i, 0)),
                  full((CONV_WIDTH, SSM_CONV_DIM)), full((1, SSM_CONV_DIM)),
                  full((1, LANES)), full((1, LANES)), full((1, SSM_INNER)),
                  full((1, SSM_INNER)), full((LANES, SSM_INNER))],
        out_specs=pl.BlockSpec((cl, SSM_INNER), lambda i: (i, 0)),
        out_shape=jax.ShapeDtypeStruct((t, SSM_INNER), BF16),
        scratch_shapes=[pltpu.VMEM((cl + 8, SSM_CONV_DIM), F32),
                        pltpu.VMEM((SSM_GROUPS, SSM_STATE, SSM_INNER // SSM_GROUPS), F32)],
        compiler_params=_params(("arbitrary",)),
        name="ssd_mixer",
    )(mid, conv_w, conv_b, dt_bias_pad, a_pad, d_skip_w, g_norm, expand)


def _merge_body(mid_ref, g_ref, gt_ref, osb_ref, ofx_ref, osm_ref, wsb_ref, wfx_ref, wsm_ref, o_ref,
                *, head_cols, gate_off, tn):
    head = _sigmoid(mid_ref[:, IN_TILE - head_cols:])
    gates = jnp.concatenate([head, g_ref[...].astype(F32), gt_ref[...].astype(F32)], axis=1)
    branches = ((osb_ref[...], wsb_ref), (ofx_ref[...], wfx_ref), (osm_ref[...], wsm_ref))
    for c in range(D_MODEL // tn):
        merged = None
        for b, (act, w_ref) in enumerate(branches):
            start = gate_off + b * D_MODEL + c * tn
            term = gates[:, start:start + tn] * jnp.dot(act, w_ref[:, c * tn:(c + 1) * tn],
                                                        preferred_element_type=F32)
            merged = term if merged is None else merged + term
        o_ref[:, c * tn:(c + 1) * tn] = merged.astype(o_ref.dtype)


def _merge(mid, gates, gates_tail, head_cols, gate_off, o_sb, o_fox, o_ssm, w_sb, w_fox, w_ssm):
    t = mid.shape[0]
    tm = 256

    def rows(width, col_block=0):
        return pl.BlockSpec((tm, width), lambda i: (i, col_block))

    def whole(shape):
        return pl.BlockSpec(shape, lambda i: (0, 0))

    return pl.pallas_call(
        functools.partial(_merge_body, head_cols=head_cols, gate_off=gate_off, tn=512),
        grid=(t // tm,),
        in_specs=[rows(IN_TILE, mid.shape[1] // IN_TILE - 1), rows(gates.shape[1]),
                  rows(gates_tail.shape[1]),
                  rows(SB_WIDTH), rows(FOX_WIDTH), rows(SSM_INNER),
                  whole(w_sb.shape), whole(w_fox.shape), whole(w_ssm.shape)],
        out_specs=rows(D_MODEL),
        out_shape=jax.ShapeDtypeStruct((t, D_MODEL), BF16),
        compiler_params=_params(("arbitrary",)),
        name="branch_merge",
    )(mid, gates, gates_tail, o_sb, o_fox, o_ssm, w_sb, w_fox, w_ssm)


def _route(logits_t, b_router, count_ref, tm):
    aff = _sigmoid(logits_t)
    sel = aff + b_router
    rows = [sel[e:e + 1, :] for e in range(N_EXPERTS)]
    best_score, best_group = None, None
    for g in range(N_EXPERT_GROUPS):
        r = rows[g * EXPERTS_PER_GROUP:(g + 1) * EXPERTS_PER_GROUP]
        score = None
        for a in range(EXPERTS_PER_GROUP):
            for b in range(a + 1, EXPERTS_PER_GROUP):
                pair = r[a] + r[b]
                score = pair if score is None else jnp.maximum(score, pair)
        if g == 0:
            best_score, best_group = score, jnp.zeros_like(score, dtype=jnp.int32)
        else:
            better = score > best_score
            best_group = jnp.where(better, g, best_group)
            best_score = jnp.where(better, score, best_score)
    masked = [jnp.where(best_group == e // EXPERTS_PER_GROUP, rows[e], -jnp.inf)
              for e in range(N_EXPERTS)]

    def arg_top(vals):
        top, idx = vals[0], jnp.zeros_like(best_group)
        for e in range(1, N_EXPERTS):
            better = vals[e] > top
            idx = jnp.where(better, e, idx)
            top = jnp.where(better, vals[e], top)
        return idx

    e1 = arg_top(masked)
    e2 = arg_top([jnp.where(e1 == e, -jnp.inf, masked[e]) for e in range(N_EXPERTS)])
    expert_id = lax.broadcasted_iota(jnp.int32, (N_EXPERTS, tm), 0)
    pick1 = expert_id == e1
    pick2 = expert_id == e2
    a1 = jnp.sum(jnp.where(pick1, aff, 0.0), axis=0, keepdims=True)
    a2 = jnp.sum(jnp.where(pick2, aff, 0.0), axis=0, keepdims=True)
    denom = a1 + a2
    chosen = jnp.logical_or(pick1, pick2)
    t_row = lax.broadcasted_iota(jnp.int32, (tm, tm), 0)
    t_col = lax.broadcasted_iota(jnp.int32, (tm, tm), 1)
    earlier = (t_row < t_col).astype(BF16)
    rank = jnp.dot(chosen.astype(BF16), earlier, preferred_element_type=F32) + count_ref[...]
    r1 = jnp.sum(jnp.where(pick1, rank, 0.0), axis=0, keepdims=True)
    r2 = jnp.sum(jnp.where(pick2, rank, 0.0), axis=0, keepdims=True)
    count_ref[...] = count_ref[...] + jnp.sum(chosen.astype(F32), axis=1, keepdims=True)
    zero = jnp.zeros_like(a1)
    return jnp.concatenate([e1.astype(F32), e2.astype(F32), r1, r2, a1 / denom, a2 / denom,
                            zero, zero], axis=0)


def _outproj_body(m_ref, w_ref, x_ref, gm_ref, g_ref, sc_ref, sh_ref, wr_ref, br_ref,
                  xo_ref, hp_ref, rt_ref, cnt_ref, count_ref, *, tm):
    @pl.when(pl.program_id(0) == 0)
    def _():
        count_ref[...] = jnp.zeros_like(count_ref)

    x_new = x_ref[...] + gm_ref[...] * jnp.dot(m_ref[...], w_ref[...], preferred_element_type=F32)
    xo_ref[...] = x_new
    h = _modnorm(x_new, g_ref[...], sc_ref[...], sh_ref[...])
    hp_ref[...] = h
    logits_t = lax.dot_general(wr_ref[...], h, (((1,), (1,)), ((), ())),
                               precision=HIGHEST, preferred_element_type=F32)
    rt_ref[...] = _route(logits_t, br_ref[...], count_ref, tm)
    cnt_ref[...] = jnp.broadcast_to(count_ref[...], cnt_ref.shape)


def _outproj_route(merged, w_out, x, gate_m, g_ffn, scale_f, shift_f, w_router_t, b_router_col):
    t, d = x.shape
    tm = ROW_TILE
    vec = pl.BlockSpec((1, d), lambda i: (0, 0))
    return pl.pallas_call(
        functools.partial(_outproj_body, tm=tm),
        grid=(t // tm,),
        in_specs=[pl.BlockSpec((tm, d), lambda i: (i, 0)),
                  pl.BlockSpec((d, d), lambda i: (0, 0)),
                  pl.BlockSpec((tm, d), lambda i: (i, 0)),
                  vec, vec, vec, vec,
                  pl.BlockSpec((N_EXPERTS, d), lambda i: (0, 0)),
                  pl.BlockSpec((N_EXPERTS, 1), lambda i: (0, 0))],
        out_specs=[pl.BlockSpec((tm, d), lambda i: (i, 0)),
                   pl.BlockSpec((tm, d), lambda i: (i, 0)),
                   pl.BlockSpec((8, tm), lambda i: (0, i)),
                   pl.BlockSpec((N_EXPERTS, LANES), lambda i: (0, 0))],
        out_shape=[jax.ShapeDtypeStruct((t, d), F32),
                   jax.ShapeDtypeStruct((t, d), F32),
                   jax.ShapeDtypeStruct((8, t), F32),
                   jax.ShapeDtypeStruct((N_EXPERTS, LANES), F32)],
        scratch_shapes=[pltpu.VMEM((N_EXPERTS, 1), F32)],
        compiler_params=_params(("arbitrary",)),
        name="outproj_route",
    )(merged, w_out, x, gate_m, g_ffn, scale_f, shift_f, w_router_t, b_router_col)


def _dispatch_body(dest_ref, h_ref, init_ref, xs_ref, sem, *, tm, t):
    del init_ref
    base = pl.program_id(0) * tm

    def row_copy(r, d):
        return pltpu.make_async_copy(h_ref.at[pl.ds(r, 1)], xs_ref.at[pl.ds(d, 1)], sem)

    def issue(r, carry):
        row_copy(r, dest_ref[base + r]).start()
        row_copy(r, dest_ref[t + base + r]).start()
        return carry

    lax.fori_loop(0, tm, issue, 0)
    for _ in range(2):
        pltpu.make_async_copy(h_ref, xs_ref.at[pl.ds(0, tm)], sem).wait()


def _dispatch(dest, h, n_rows):
    t, d = h.shape
    tm = ROW_TILE
    init = jnp.zeros((n_rows, d), F32)
    return pl.pallas_call(
        functools.partial(_dispatch_body, tm=tm, t=t),
        grid_spec=pltpu.PrefetchScalarGridSpec(
            num_scalar_prefetch=1,
            grid=(t // tm,),
            in_specs=[pl.BlockSpec((tm, d), lambda i, dest: (i, 0)),
                      pl.BlockSpec(memory_space=pl.ANY)],
            out_specs=pl.BlockSpec(memory_space=pl.ANY),
            scratch_shapes=[pltpu.SemaphoreType.DMA(())]),
        out_shape=jax.ShapeDtypeStruct((n_rows, d), F32),
        input_output_aliases={2: 0},
        compiler_params=_params(("arbitrary",)),
        name="moe_dispatch",
    )(dest, h, init)


def _expert_body(te_ref, nu_ref, xs_ref, wg_ref, wu_ref, wd_ref, o_ref):
    del te_ref

    @pl.when(pl.program_id(0) < nu_ref[0])
    def _():
        xb = xs_ref[...].astype(BF16)
        gate = jnp.dot(xb, wg_ref[...], preferred_element_type=F32)
        up = jnp.dot(xb, wu_ref[...], preferred_element_type=F32)
        act = (gate * _sigmoid(gate) * up).astype(BF16)
        o_ref[...] = jnp.dot(act, wd_ref[...], preferred_element_type=F32)

    @pl.when(pl.program_id(0) >= nu_ref[0])
    def _():
        o_ref[...] = jnp.zeros_like(o_ref)


def _experts(tile_expert, n_used, xs, layer, w_gate, w_up, w_down):
    n_rows = xs.shape[0]
    te = EXPERT_TILE

    def row_map(i, tex, nu):
        return (jnp.maximum(jnp.minimum(i, nu[0] - 1), 0), 0)

    def w_map(i, tex, nu):
        return (layer, tex[i], 0, 0)

    return pl.pallas_call(
        _expert_body,
        grid_spec=pltpu.PrefetchScalarGridSpec(
            num_scalar_prefetch=2,
            grid=(n_rows // te,),
            in_specs=[pl.BlockSpec((te, D_MODEL), row_map),
                      pl.BlockSpec((None, None, D_MODEL, D_FF_EXPERT), w_map),
                      pl.BlockSpec((None, None, D_MODEL, D_FF_EXPERT), w_map),
                      pl.BlockSpec((None, None, D_FF_EXPERT, D_MODEL), w_map)],
            out_specs=pl.BlockSpec((te, D_MODEL), lambda i, tex, nu: (i, 0))),
        out_shape=jax.ShapeDtypeStruct((n_rows, D_MODEL), F32),
        compiler_params=_params(("arbitrary",)),
        name="moe_experts",
    )(tile_expert, n_used, xs, w_gate, w_up, w_down)


def _combine_body(dest_ref, ys_ref, x_ref, w_ref, gf_ref, o_ref, a_buf, b_buf, sem, *, tm, t):
    base = pl.program_id(0) * tm

    def row_copy(d, buf, r):
        return pltpu.make_async_copy(ys_ref.at[pl.ds(d, 1)], buf.at[pl.ds(r, 1)], sem)

    def issue(r, carry):
        row_copy(dest_ref[base + r], a_buf, r).start()
        row_copy(dest_ref[t + base + r], b_buf, r).start()
        return carry

    lax.fori_loop(0, tm, issue, 0)
    pltpu.make_async_copy(ys_ref.at[pl.ds(0, tm)], a_buf, sem).wait()
    pltpu.make_async_copy(ys_ref.at[pl.ds(0, tm)], b_buf, sem).wait()
    w = w_ref[...]
    y = w[:, 0:1] * a_buf[...] + w[:, 1:2] * b_buf[...]
    o_ref[...] = x_ref[...] + gf_ref[...] * y


def _combine(dest, ys, x, w_cols, gate_f):
    t, d = x.shape
    tm = COMBINE_TILE
    return pl.pallas_call(
        functools.partial(_combine_body, tm=tm, t=t),
        grid_spec=pltpu.PrefetchScalarGridSpec(
            num_scalar_prefetch=1,
            grid=(t // tm,),
            in_specs=[pl.BlockSpec(memory_space=pl.ANY),
                      pl.BlockSpec((tm, d), lambda i, dest: (i, 0)),
                      pl.BlockSpec((tm, 8), lambda i, dest: (i, 0)),
                      pl.BlockSpec((1, d), lambda i, dest: (0, 0))],
            out_specs=pl.BlockSpec((tm, d), lambda i, dest: (i, 0)),
            scratch_shapes=[pltpu.VMEM((tm, d), F32), pltpu.VMEM((tm, d), F32),
                            pltpu.SemaphoreType.DMA(())]),
        out_shape=jax.ShapeDtypeStruct((t, d), F32),
        compiler_params=_params(("arbitrary",)),
        name="moe_combine",
    )(dest, ys, x, w_cols, gate_f)


def _pad_lanes(v, fill=0.0):
    return jnp.pad(v.astype(F32), (0, LANES - v.shape[0]), constant_values=fill).reshape(1, LANES)


def _routing_tables(route, counts, t):
    te = EXPERT_TILE
    n_tiles = (2 * t) // te + N_EXPERTS
    cnt = counts[:, 0].astype(jnp.int32)
    padded = ((cnt + te - 1) // te) * te
    ends = jnp.cumsum(padded)
    offsets = ends - padded
    e1 = route[0].astype(jnp.int32)
    e2 = route[1].astype(jnp.int32)
    d1 = offsets[e1] + route[2].astype(jnp.int32)
    d2 = offsets[e2] + route[3].astype(jnp.int32)
    dest = jnp.concatenate([d1, d2]).astype(jnp.int32)
    n_used = (ends[-1] // te).astype(jnp.int32)
    tile_start = jnp.arange(n_tiles, dtype=jnp.int32) * te
    tile_expert = jnp.sum((tile_start[:, None] >= ends[None, :]).astype(jnp.int32), axis=1)
    last_expert = tile_expert[jnp.maximum(n_used - 1, 0)]
    tile_expert = jnp.where(jnp.arange(n_tiles) < n_used, tile_expert, last_expert)
    tile_expert = jnp.minimum(tile_expert, N_EXPERTS - 1).astype(jnp.int32)
    w_cols = jnp.transpose(route)
    w_cols = jnp.concatenate([w_cols[:, 4:6], w_cols[:, 0:6]], axis=1)
    return dest, tile_expert, n_used.reshape(1), w_cols, n_tiles * te


def kernel(x, c, w_ada, b_ada, g_norm_mix, w_in, b_fgate, g_q_fox, g_k_fox, conv_w, conv_b,
           dt_bias, a_log, d_skip, g_ssm_norm, w_branch_sb, w_branch_fox, w_branch_ssm, w_out,
           g_norm_ffn, w_router, b_router, w_e_gate, w_e_up, w_e_down):
    bsz, t, d = x.shape
    assert bsz == 1 and d == D_MODEL
    n_layers = w_ada.shape[0]
    xt = x.reshape(t, d)

    mod = _ada(c, w_ada, b_ada)
    expand = (jnp.arange(SSM_INNER)[None, :] // SSM_HEAD_DIM
              == jnp.arange(LANES)[:, None]).astype(F32)
    w_router_t = jnp.transpose(w_router)
    b_router_col = b_router.reshape(N_EXPERTS, 1)

    o_ff = 3 * SB_WIDTH + 3 * FOX_WIDTH
    o_z = o_ff + FOX_HEADS
    o_xbc = o_z + SSM_INNER
    o_dt = o_xbc + SSM_CONV_DIM
    o_gate = o_dt + SSM_HEADS
    width = w_in.shape[2]
    assert o_ff % IN_TILE == 0 and o_gate + N_BRANCH * d == width
    qkv_tiles = o_ff // IN_TILE
    mid_end = -(-o_gate // IN_TILE) * IN_TILE
    mid_tiles = (mid_end - o_ff) // IN_TILE
    full_end = width // IN_TILE * IN_TILE
    head_start = o_gate // LANES * LANES
    tail_pad = -(-(width - full_end) // LANES) * LANES
    q_scale = jnp.concatenate([jnp.full((SB_WIDTH,), HEAD_DIM ** -0.5 * LOG2E, F32),
                               jnp.ones((o_ff - SB_WIDTH,), F32)]).reshape(1, o_ff)
    w_gate_b, w_up_b, w_down_b = _cast_bf16(w_e_gate), _cast_bf16(w_e_up), _cast_bf16(w_e_down)
    w_tail = jnp.pad(lax.slice_in_dim(w_in, full_end, width, axis=2),
                     ((0, 0), (0, 0), (0, tail_pad - (width - full_end))))

    for layer in range(n_layers):
        m = mod[layer].reshape(6, 1, d)
        shift_m, scale_m, gate_m, shift_f, scale_f, gate_f = (m[i] for i in range(6))

        h = _norm(xt, g_norm_mix[layer].reshape(1, d), scale_m, shift_m)
        qkv = _proj(h, w_in, layer, 0, qkv_tiles, BF16, col_scale=q_scale, name="in_proj_qkv")
        mid = _proj(h, w_in, layer, qkv_tiles, mid_tiles, F32, name="in_proj_mid")
        gates = _proj(h, w_in, layer, mid_end // IN_TILE, (full_end - mid_end) // IN_TILE, BF16,
                      act="sigmoid", name="in_proj_gates")
        gates_tail = _mm(h, w_tail[layer], BF16, tail_pad, act="sigmoid", name="in_proj_gates_tail")

        o_sb = _sb_attention(qkv)
        qn, kn, f_col = _fox_prep(qkv, mid, _pad_lanes(b_fgate[layer]),
                                  g_q_fox[layer].reshape(1, HEAD_DIM),
                                  g_k_fox[layer].reshape(1, HEAD_DIM))
        f_row = jnp.transpose(f_col[:, :8])
        o_fox = _fox_attention(qn, kn, qkv, f_col, f_row)
        o_ssm = _ssd(mid, (o_z - o_ff, o_xbc - o_ff, o_dt - o_ff), conv_w[layer],
                     conv_b[layer].reshape(1, SSM_CONV_DIM),
                     _pad_lanes(dt_bias[layer]), _pad_lanes(-jnp.exp(a_log[layer].astype(F32))),
                     jnp.repeat(d_skip[layer].astype(F32), SSM_HEAD_DIM).reshape(1, SSM_INNER),
                     g_ssm_norm[layer].reshape(1, SSM_INNER), expand)

        merged = _merge(mid, gates, gates_tail, mid_end - head_start, o_gate - head_start,
                        o_sb, o_fox, o_ssm, w_branch_sb[layer].astype(BF16),
                        w_branch_fox[layer].astype(BF16), w_branch_ssm[layer].astype(BF16))
        xt, h_moe, route, counts = _outproj_route(
            merged, w_out[layer].astype(BF16), xt, gate_m, g_norm_ffn[layer].reshape(1, d),
            scale_f, shift_f, w_router_t, b_router_col)

        dest, tile_expert, n_used, w_cols, n_rows = _routing_tables(route, counts, t)
        xs = _dispatch(dest, h_moe, n_rows)
        ys = _experts(tile_expert, n_used, xs, layer, w_gate_b, w_up_b, w_down_b)
        xt = _combine(dest, ys, xt, w_cols, gate_f)
    return xt.reshape(bsz, t, d)
```

```python
import functools

import jax
import jax.numpy as jnp
from jax import lax
from jax.experimental import pallas as pl
from jax.experimental.pallas import tpu as pltpu

F32 = jnp.float32
BF16 = jnp.bfloat16
HIGHEST = lax.Precision.HIGHEST
LOG2E = 1.4426950408889634

D_MODEL = 2048
EPS = 1e-6
HEAD_DIM = 128
SB_HEADS = 4
FOX_HEADS = 4
SB_WIDTH = SB_HEADS * HEAD_DIM
FOX_WIDTH = FOX_HEADS * HEAD_DIM
SSM_HEAD_DIM = 64
SSM_HEADS = 16
SSM_INNER = SSM_HEADS * SSM_HEAD_DIM
SSM_GROUPS = 2
SSM_STATE = 128
CONV_WIDTH = 4
SSM_CONV_DIM = SSM_INNER + 2 * SSM_GROUPS * SSM_STATE
N_BRANCH = 3
N_EXPERTS = 16
N_EXPERT_GROUPS = 4
EXPERTS_PER_GROUP = N_EXPERTS // N_EXPERT_GROUPS
D_FF_EXPERT = 1024

LANES = 128
V7X_VMEM_BYTES = 64 * 1024 * 1024
VMEM_LIMIT = 48 * 1024 * 1024

ROW_TILE = 512
IN_TILE = 1024
ATTN_TILE = 256
ATTN_SUB = 4
KEY_UNROLL = 4
SSD_CHUNK = 128
EXPERT_TILE = 256
COMBINE_TILE = 256


def _params(sem, vmem=VMEM_LIMIT):
    return pltpu.CompilerParams(dimension_semantics=sem, vmem_limit_bytes=vmem)


def _sigmoid(x):
    return 1.0 / (1.0 + jnp.exp(-x))


def _log_sigmoid(x):
    return jnp.minimum(x, 0.0) - jnp.log(1.0 + jnp.exp(-jnp.abs(x)))


def _softplus(x):
    return jnp.maximum(x, 0.0) + jnp.log(1.0 + jnp.exp(-jnp.abs(x)))


def _ada_body(c_ref, w_ref, b_ref, o_ref):
    k = pl.program_id(1)

    @pl.when(k == 0)
    def _():
        o_ref[0] = b_ref[0]

    c = c_ref[...]
    cond = c * _sigmoid(c)
    parts = [jnp.sum(w_ref[0, :, j * LANES:(j + 1) * LANES] * cond, axis=0, keepdims=True)
             for j in range(w_ref.shape[2] // LANES)]
    o_ref[0] += jnp.concatenate(parts, axis=1)


def _ada(c, w_ada, b_ada):
    n_layers, d, n = w_ada.shape
    tk = 256
    c_lanes = jnp.broadcast_to(c.reshape(d, 1), (d, LANES))
    out = pl.pallas_call(
        _ada_body,
        grid=(n_layers, d // tk),
        in_specs=[pl.BlockSpec((tk, LANES), lambda l, k: (k, 0)),
                  pl.BlockSpec((1, tk, n), lambda l, k: (l, k, 0)),
                  pl.BlockSpec((1, 1, n), lambda l, k: (l, 0, 0))],
        out_specs=pl.BlockSpec((1, 1, n), lambda l, k: (l, 0, 0)),
        out_shape=jax.ShapeDtypeStruct((n_layers, 1, n), F32),
        compiler_params=_params(("arbitrary", "arbitrary")),
        name="ada_mod",
    )(c_lanes, w_ada, b_ada.reshape(n_layers, 1, n))
    return out[:, 0, :]


def _modnorm(x, g, scale, shift):
    y = x * lax.rsqrt(jnp.mean(x * x, axis=-1, keepdims=True) + EPS) * g
    return y * (1.0 + scale) + shift


def _norm_body(x_ref, g_ref, sc_ref, sh_ref, o_ref):
    o_ref[...] = _modnorm(x_ref[...], g_ref[...], sc_ref[...], sh_ref[...]).astype(BF16)


def _norm(x, g, scale, shift):
    t, d = x.shape
    vec = pl.BlockSpec((1, d), lambda i: (0, 0))
    return pl.pallas_call(
        _norm_body,
        grid=(t // ROW_TILE,),
        in_specs=[pl.BlockSpec((ROW_TILE, d), lambda i: (i, 0)), vec, vec, vec],
        out_specs=pl.BlockSpec((ROW_TILE, d), lambda i: (i, 0)),
        out_shape=jax.ShapeDtypeStruct((t, d), BF16),
        compiler_params=_params(("arbitrary",)),
        name="mod_norm",
    )(x, g, scale, shift)


def _mm_body(a_ref, w_ref, o_ref, *, act):
    r = jnp.dot(a_ref[...], w_ref[...].astype(BF16), preferred_element_type=F32)
    if act == "sigmoid":
        r = _sigmoid(r)
    o_ref[...] = r.astype(o_ref.dtype)


def _mm(a, w, out_dtype, tn, act=None, tm=1024, name="matmul"):
    m, k = a.shape
    n = w.shape[1]
    tm = min(tm, m)
    return pl.pallas_call(
        functools.partial(_mm_body, act=act),
        grid=(m // tm, n // tn),
        in_specs=[pl.BlockSpec((tm, k), lambda i, j: (i, 0)),
                  pl.BlockSpec((k, tn), lambda i, j: (0, j))],
        out_specs=pl.BlockSpec((tm, tn), lambda i, j: (i, j)),
        out_shape=jax.ShapeDtypeStruct((m, n), out_dtype),
        compiler_params=_params(("arbitrary", "arbitrary")),
        name=name,
    )(a, w)


def _cast_body(w_ref, o_ref):
    o_ref[...] = w_ref[...].astype(o_ref.dtype)


def _cast_bf16(w):
    shape = w.shape
    n = shape[-1]
    rows = w.size // n
    tr = min(rows, (2 * 1024 * 1024) // n)
    out = pl.pallas_call(
        _cast_body,
        grid=(rows // tr,),
        in_specs=[pl.BlockSpec((tr, n), lambda i: (i, 0))],
        out_specs=pl.BlockSpec((tr, n), lambda i: (i, 0)),
        out_shape=jax.ShapeDtypeStruct((rows, n), BF16),
        compiler_params=_params(("arbitrary",)),
        name="cast_bf16",
    )(w.reshape(rows, n))
    return out.reshape(shape)


def _proj_body(*refs, act, scaled):
    if scaled:
        h_ref, w_ref, sc_ref, o_ref, wb_ref = refs
    else:
        h_ref, w_ref, o_ref, wb_ref = refs

    @pl.when(pl.program_id(1) == 0)
    def _():
        wb_ref[...] = w_ref[0].astype(BF16)

    r = jnp.dot(h_ref[...], wb_ref[...], preferred_element_type=F32)
    if scaled:
        r = r * sc_ref[...]
    if act == "sigmoid":
        r = _sigmoid(r)
    o_ref[...] = r.astype(o_ref.dtype)


def _proj(h, w_in, layer, first_tile, n_tiles, out_dtype, act=None, col_scale=None, name="in_proj"):
    t, k = h.shape
    tm, tn = 1024, IN_TILE
    assert (first_tile + n_tiles) * tn <= w_in.shape[2]
    in_specs = [pl.BlockSpec((tm, k), lambda j, i: (i, 0)),
                pl.BlockSpec((1, k, tn), lambda j, i: (layer, 0, first_tile + j))]
    args = [h, w_in]
    if col_scale is not None:
        in_specs.append(pl.BlockSpec((1, tn), lambda j, i: (0, j)))
        args.append(col_scale)
    return pl.pallas_call(
        functools.partial(_proj_body, act=act, scaled=col_scale is not None),
        grid=(n_tiles, t // tm),
        in_specs=in_specs,
        out_specs=pl.BlockSpec((tm, tn), lambda j, i: (i, j)),
        out_shape=jax.ShapeDtypeStruct((t, n_tiles * tn), out_dtype),
        scratch_shapes=[pltpu.VMEM((k, tn), BF16)],
        compiler_params=_params(("arbitrary", "arbitrary")),
        name=name,
    )(*args)


def _fox_prep_body(q_ref, k_ref, f_ref, bf_ref, gq_ref, gk_ref,
                   qo_ref, ko_ref, fo_ref, carry_ref, *, tm):
    @pl.when(pl.program_id(0) == 0)
    def _():
        carry_ref[...] = jnp.zeros_like(carry_ref)

    scale = HEAD_DIM ** -0.5 * LOG2E
    for h in range(FOX_HEADS):
        sl = slice(h * HEAD_DIM, (h + 1) * HEAD_DIM)
        q = q_ref[:, sl].astype(F32)
        qn = q * lax.rsqrt(jnp.mean(q * q, axis=-1, keepdims=True) + EPS) * gq_ref[...]
        qo_ref[:, sl] = (qn * scale).astype(BF16)
        k = k_ref[:, sl].astype(F32)
        kn = k * lax.rsqrt(jnp.mean(k * k, axis=-1, keepdims=True) + EPS) * gk_ref[...]
        ko_ref[:, sl] = kn.astype(BF16)

    log_f = _log_sigmoid(f_ref[...] + bf_ref[...])
    row = lax.broadcasted_iota(jnp.int32, (tm, tm), 0)
    col = lax.broadcasted_iota(jnp.int32, (tm, tm), 1)
    tri = (col <= row).astype(F32)
    cum = jnp.dot(tri, log_f, precision=HIGHEST, preferred_element_type=F32) + carry_ref[...]
    fo_ref[...] = cum * LOG2E
    carry_ref[...] = cum[tm - 1:tm, :]


def _fox_prep(qkv, misc, b_fgate_pad, g_q, g_k):
    t = qkv.shape[0]
    tm = ROW_TILE
    q_blk = 3 * SB_WIDTH // FOX_WIDTH
    vec = pl.BlockSpec((1, LANES), lambda i: (0, 0))
    return pl.pallas_call(
        functools.partial(_fox_prep_body, tm=tm),
        grid=(t // tm,),
        in_specs=[pl.BlockSpec((tm, FOX_WIDTH), lambda i: (i, q_blk)),
                  pl.BlockSpec((tm, FOX_WIDTH), lambda i: (i, q_blk + 1)),
                  pl.BlockSpec((tm, LANES), lambda i: (i, 0)),
                  vec, vec, vec],
        out_specs=[pl.BlockSpec((tm, FOX_WIDTH), lambda i: (i, 0)),
                   pl.BlockSpec((tm, FOX_WIDTH), lambda i: (i, 0)),
                   pl.BlockSpec((tm, LANES), lambda i: (i, 0))],
        out_shape=[jax.ShapeDtypeStruct((t, FOX_WIDTH), BF16),
                   jax.ShapeDtypeStruct((t, FOX_WIDTH), BF16),
                   jax.ShapeDtypeStruct((t, LANES), F32)],
        scratch_shapes=[pltpu.VMEM((1, LANES), F32)],
        compiler_params=_params(("arbitrary",)),
        name="fox_prep",
    )(qkv, qkv, misc, b_fgate_pad, g_q, g_k)


def _sb_body(q_ref, k_ref, v_ref, o_ref, *, tb, nsub):
    qb = pl.program_id(1)
    qs = [q_ref[a * tb:(a + 1) * tb, :] for a in range(nsub)]
    row = lax.broadcasted_iota(jnp.int32, (tb, tb), 0)
    col = lax.broadcasted_iota(jnp.int32, (tb, tb), 1)
    ones = (row >= col).astype(BF16)
    suffix_ones = jnp.concatenate([ones, ones], axis=0)
    past = col < row

    def load(kb):
        ks = pl.multiple_of(kb * tb, tb)
        return k_ref[pl.ds(ks, tb), :], v_ref[pl.ds(ks, tb), :]

    def sweep(k, v, chains):
        us = [lax.dot_general(q, k, (((1,), (1,)), ((), ())), preferred_element_type=F32)
              for q, _, _, _ in chains]
        splits = []
        for u, (_, _, _, diagonal) in zip(us, chains):
            drop = jnp.maximum(u, 0.0) + jnp.log2(1.0 + jnp.exp2(-jnp.abs(u)))
            if diagonal:
                drop = jnp.where(past, drop, 0.0)
            hi = drop.astype(BF16)
            lo = (drop - hi.astype(F32)).astype(BF16)
            splits.append(jnp.concatenate([hi, lo], axis=1))
        sufs = [jnp.dot(s, suffix_ones, preferred_element_type=F32) for s in splits]
        ws = []
        for u, suf, (_, carry, _, diagonal) in zip(us, sufs, chains):
            w = jnp.exp2(u - suf - carry)
            if diagonal:
                w = jnp.where(past, w, 0.0)
            ws.append(w.astype(BF16))
        return [(carry + suf[:, 0:1], acc + jnp.dot(w, v, preferred_element_type=F32))
                for w, suf, (_, carry, acc, _) in zip(ws, sufs, chains)]

    state = [(jnp.zeros((tb, 1), F32), jnp.zeros((tb, HEAD_DIM), F32)) for _ in range(nsub)]
    for j in reversed(range(nsub)):
        k, v = load(qb * nsub + j)
        new = sweep(k, v, [(qs[a], state[a][0], state[a][1], a == j) for a in range(j, nsub)])
        state[j:] = new

    def step(i, c):
        for j in range(KEY_UNROLL):
            k, v = load(qb * nsub - 1 - (i * KEY_UNROLL + j))
            c = tuple(sweep(k, v, [(qs[a], c[a][0], c[a][1], False) for a in range(nsub)]))
        return c

    state = lax.fori_loop(0, qb * nsub // KEY_UNROLL, step, tuple(state))
    for a in range(nsub):
        o_ref[a * tb:(a + 1) * tb, :] = state[a][1].astype(o_ref.dtype)


def _sb_attention(qkv):
    t = qkv.shape[0]
    tb, nsub = ATTN_TILE, ATTN_SUB
    return pl.pallas_call(
        functools.partial(_sb_body, tb=tb, nsub=nsub),
        grid=(SB_HEADS, t // (tb * nsub)),
        in_specs=[pl.BlockSpec((tb * nsub, HEAD_DIM), lambda h, i: (i, h)),
                  pl.BlockSpec((t, HEAD_DIM), lambda h, i: (0, SB_HEADS + h)),
                  pl.BlockSpec((t, HEAD_DIM), lambda h, i: (0, 2 * SB_HEADS + h))],
        out_specs=pl.BlockSpec((tb * nsub, HEAD_DIM), lambda h, i: (i, h)),
        out_shape=jax.ShapeDtypeStruct((t, SB_WIDTH), BF16),
        compiler_params=_params(("arbitrary", "arbitrary")),
        name="sb_attention",
    )(qkv, qkv, qkv)


def _fox_body(q_ref, k_ref, v_ref, fcol_ref, frow_ref, o_ref, *, tb, nsub):
    h = pl.program_id(0)
    qb = pl.program_id(1)
    lane = lax.broadcasted_iota(jnp.int32, (tb, LANES), 1)
    qs, f_qs = [], []
    for a in range(nsub):
        qs.append(q_ref[a * tb:(a + 1) * tb, :])
        f_qs.append(jnp.sum(jnp.where(lane == h, fcol_ref[a * tb:(a + 1) * tb, :], 0.0),
                            axis=-1, keepdims=True))
    row = lax.broadcasted_iota(jnp.int32, (tb, tb), 0)
    col = lax.broadcasted_iota(jnp.int32, (tb, tb), 1)
    causal = col <= row

    def load(kb):
        ks = pl.multiple_of(kb * tb, tb)
        return (k_ref[pl.ds(ks, tb), :], v_ref[pl.ds(ks, tb), :],
                frow_ref[pl.ds(h, 1), pl.ds(ks, tb)])

    def sweep(k, v, f_k, chains):
        ss = []
        for a, _, diagonal in chains:
            s = lax.dot_general(qs[a], k, (((1,), (1,)), ((), ())), preferred_element_type=F32) - f_k
            ss.append(jnp.where(causal, s, -jnp.inf) if diagonal else s)
        ms, ps, sums = [], [], []
        for s, (a, (m, _, _), _) in zip(ss, chains):
            m_new = jnp.maximum(m, jnp.max(s, axis=-1, keepdims=True) + f_qs[a])
            p = jnp.exp2(s - (m_new - f_qs[a]))
            ms.append(m_new)
            sums.append(jnp.sum(p, axis=-1, keepdims=True))
            ps.append(p.astype(BF16))
        out = []
        for p, p_sum, m_new, (_, (m, l, acc), _) in zip(ps, sums, ms, chains):
            alpha = jnp.exp2(m - m_new)
            out.append((m_new, alpha * l + p_sum,
                        alpha * acc + jnp.dot(p, v, preferred_element_type=F32)))
        return out

    def step(i, c):
        for j in range(KEY_UNROLL):
            k, v, f_k = load(i * KEY_UNROLL + j)
            c = tuple(sweep(k, v, f_k, [(a, c[a], False) for a in range(nsub)]))
        return c

    init = tuple((jnp.full((tb, 1), -jnp.inf, F32), jnp.zeros((tb, 1), F32),
                  jnp.zeros((tb, HEAD_DIM), F32)) for _ in range(nsub))
    state = list(lax.fori_loop(0, qb * nsub // KEY_UNROLL, step, init))
    for j in range(nsub):
        k, v, f_k = load(qb * nsub + j)
        state[j:] = sweep(k, v, f_k, [(a, state[a], a == j) for a in range(j, nsub)])
    for a in range(nsub):
        o_ref[a * tb:(a + 1) * tb, :] = (state[a][2] / state[a][1]).astype(o_ref.dtype)


def _fox_attention(qn, kn, qkv, f_col, f_row):
    t = qn.shape[0]
    tb, nsub = ATTN_TILE, ATTN_SUB
    v_blk = (3 * SB_WIDTH + 2 * FOX_WIDTH) // HEAD_DIM
    return pl.pallas_call(
        functools.partial(_fox_body, tb=tb, nsub=nsub),
        grid=(FOX_HEADS, t // (tb * nsub)),
        in_specs=[pl.BlockSpec((tb * nsub, HEAD_DIM), lambda h, i: (i, h)),
                  pl.BlockSpec((t, HEAD_DIM), lambda h, i: (0, h)),
                  pl.BlockSpec((t, HEAD_DIM), lambda h, i: (0, v_blk + h)),
                  pl.BlockSpec((tb * nsub, LANES), lambda h, i: (i, 0)),
                  pl.BlockSpec((8, t), lambda h, i: (0, 0))],
        out_specs=pl.BlockSpec((tb * nsub, HEAD_DIM), lambda h, i: (i, h)),
        out_shape=jax.ShapeDtypeStruct((t, FOX_WIDTH), BF16),
        compiler_params=_params(("arbitrary", "arbitrary")),
        name="fox_attention",
    )(qn, kn, qkv, f_col, f_row)


def _ssd_body(mid_ref, cw_ref, cb_ref, dtb_ref, a_ref, dsk_ref, gn_ref, ex_ref,
              o_ref, ext_ref, state_ref, *, cl, off_z, off_xbc, off_dt):
    hg = SSM_HEADS // SSM_GROUPS
    gw = hg * SSM_HEAD_DIM

    @pl.when(pl.program_id(0) == 0)
    def _():
        ext_ref[0:8, :] = jnp.zeros((8, SSM_CONV_DIM), F32)
        state_ref[...] = jnp.zeros_like(state_ref)

    z = mid_ref[:, off_z:off_z + SSM_INNER]
    xbc_raw = mid_ref[:, off_xbc:off_xbc + SSM_CONV_DIM]
    lane = lax.broadcasted_iota(jnp.int32, (cl, LANES), 1)
    dt_raw = jnp.where(lane < SSM_HEADS, mid_ref[:, off_dt:off_dt + LANES], 0.0)

    ext_ref[8:8 + cl, :] = xbc_raw
    conv = cb_ref[...] + cw_ref[0:1, :] * ext_ref[pl.ds(8 - (CONV_WIDTH - 1), cl), :]
    for w in range(1, CONV_WIDTH):
        conv = conv + cw_ref[w:w + 1, :] * ext_ref[pl.ds(8 - (CONV_WIDTH - 1) + w, cl), :]
    ext_ref[0:8, :] = xbc_raw[cl - 8:cl, :]
    xbc = conv * _sigmoid(conv)
    xs = xbc[:, :SSM_INNER]
    b_mat = xbc[:, SSM_INNER:SSM_INNER + SSM_GROUPS * SSM_STATE]
    c_mat = xbc[:, SSM_INNER + SSM_GROUPS * SSM_STATE:]

    dt = _softplus(dt_raw + dtb_ref[...])
    log_a = dt * a_ref[...]
    row = lax.broadcasted_iota(jnp.int32, (cl, cl), 0)
    col = lax.broadcasted_iota(jnp.int32, (cl, cl), 1)
    causal = col <= row
    a_cum = jnp.dot(causal.astype(F32), log_a, precision=HIGHEST, preferred_element_type=F32)
    a_cum_t = a_cum.T
    expand = ex_ref[...]

    def widen(v):
        return jnp.dot(v, expand, precision=HIGHEST, preferred_element_type=F32)

    dt_w = widen(dt)
    a_cum_w = widen(a_cum)
    a_last_w = a_cum_w[cl - 1:cl, :]
    to_end_w = jnp.exp(a_last_w - a_cum_w)
    from_start_w = jnp.exp(a_cum_w)
    chunk_decay_w = jnp.exp(a_last_w)
    xc = xs * dt_w
    xw = (xc * to_end_w).astype(BF16)
    xc_b = xc.astype(BF16)
    low_half = lane < SSM_HEAD_DIM

    y_parts = []
    for g in range(SSM_GROUPS):
        bg = b_mat[:, g * SSM_STATE:(g + 1) * SSM_STATE].astype(BF16)
        cg = c_mat[:, g * SSM_STATE:(g + 1) * SSM_STATE].astype(BF16)
        cb = lax.dot_general(cg, bg, (((1,), (1,)), ((), ())), preferred_element_type=F32)
        state = state_ref[g]
        y_off = jnp.dot(cg, state.astype(BF16), preferred_element_type=F32)
        y_off = y_off * from_start_w[:, g * gw:(g + 1) * gw]
        new_state = lax.dot_general(bg, xw[:, g * gw:(g + 1) * gw], (((0,), (0,)), ((), ())),
                                    preferred_element_type=F32)
        state_ref[g] = state * chunk_decay_w[:, g * gw:(g + 1) * gw] + new_state
        for pair in range(hg // 2):
            h0 = g * hg + 2 * pair
            c0 = h0 * SSM_HEAD_DIM
            x_pair = xc_b[:, c0:c0 + LANES]
            y_pair = None
            for k in range(2):
                h = h0 + k
                seg = a_cum[:, h:h + 1] - a_cum_t[h:h + 1, :]
                decay = jnp.where(causal, jnp.exp(jnp.where(causal, seg, 0.0)), 0.0)
                m = (cb * decay).astype(BF16)
                x_half = jnp.where(low_half if k == 0 else jnp.logical_not(low_half),
                                   x_pair, jnp.zeros_like(x_pair))
                part = jnp.dot(m, x_half, preferred_element_type=F32)
                y_pair = part if y_pair is None else y_pair + part
            y_parts.append(y_pair + y_off[:, c0 - g * gw:c0 - g * gw + LANES])
    y = jnp.concatenate(y_parts, axis=-1) + dsk_ref[...] * xs
    yz = y * (z * _sigmoid(z))
    out = yz * lax.rsqrt(jnp.mean(yz * yz, axis=-1, keepdims=True) + EPS) * gn_ref[...]
    o_ref[...] = out.astype(o_ref.dtype)


def _ssd(mid, offsets, conv_w, conv_b, dt_bias_pad, a_pad, d_skip_w, g_norm, expand):
    t, width = mid.shape
    cl = SSD_CHUNK
    off_z, off_xbc, off_dt = offsets

    def full(shape):
        return pl.BlockSpec(shape, lambda i: (0,) * len(shape))

    return pl.pallas_call(
        functools.partial(_ssd_body, cl=cl, off_z=off_z, off_xbc=off_xbc, off_dt=off_dt),
        grid=(t // cl,),
        in_specs=[pl.BlockSpec((cl, width), lambda i: (i, 0)),
                  full((CONV_WIDTH, SSM_CONV_DIM)), full((1, SSM_CONV_DIM)),
                  full((1, LANES)), full((1, LANES)), full((1, SSM_INNER)),
                  full((1, SSM_INNER)), full((LANES, SSM_INNER))],
        out_specs=pl.BlockSpec((cl, SSM_INNER), lambda i: (i, 0)),
        out_shape=jax.ShapeDtypeStruct((t, SSM_INNER), BF16),
        scratch_shapes=[pltpu.VMEM((cl + 8, SSM_CONV_DIM), F32),
                        pltpu.VMEM((SSM_GROUPS, SSM_STATE, SSM_INNER // SSM_GROUPS), F32)],
        compiler_params=_params(("arbitrary",)),
        name="ssd_mixer",
    )(mid, conv_w, conv_b, dt_bias_pad, a_pad, d_skip_w, g_norm, expand)


def _merge_body(mid_ref, g_ref, gt_ref, osb_ref, ofx_ref, osm_ref, wsb_ref, wfx_ref, wsm_ref, o_ref,
                *, head_cols, gate_off, tn):
    head = _sigmoid(mid_ref[:, IN_TILE - head_cols:])
    gates = jnp.concatenate([head, g_ref[...].astype(F32), gt_ref[...].astype(F32)], axis=1)
    branches = ((osb_ref[...], wsb_ref), (ofx_ref[...], wfx_ref), (osm_ref[...], wsm_ref))
    for c in range(D_MODEL // tn):
        merged = None
        for b, (act, w_ref) in enumerate(branches):
            start = gate_off + b * D_MODEL + c * tn
            term = gates[:, start:start + tn] * jnp.dot(act, w_ref[:, c * tn:(c + 1) * tn],
                                                        preferred_element_type=F32)
            merged = term if merged is None else merged + term
        o_ref[:, c * tn:(c + 1) * tn] = merged.astype(o_ref.dtype)


def _merge(mid, gates, gates_tail, head_cols, gate_off, o_sb, o_fox, o_ssm, w_sb, w_fox, w_ssm):
    t = mid.shape[0]
    tm = 256

    def rows(width, col_block=0):
        return pl.BlockSpec((tm, width), lambda i: (i, col_block))

    def whole(shape):
        return pl.BlockSpec(shape, lambda i: (0, 0))

    return pl.pallas_call(
        functools.partial(_merge_body, head_cols=head_cols, gate_off=gate_off, tn=512),
        grid=(t // tm,),
        in_specs=[rows(IN_TILE, mid.shape[1] // IN_TILE - 1), rows(gates.shape[1]),
                  rows(gates_tail.shape[1]),
                  rows(SB_WIDTH), rows(FOX_WIDTH), rows(SSM_INNER),
                  whole(w_sb.shape), whole(w_fox.shape), whole(w_ssm.shape)],
        out_specs=rows(D_MODEL),
        out_shape=jax.ShapeDtypeStruct((t, D_MODEL), BF16),
        compiler_params=_params(("arbitrary",)),
        name="branch_merge",
    )(mid, gates, gates_tail, o_sb, o_fox, o_ssm, w_sb, w_fox, w_ssm)


def _route(logits_t, b_router, count_ref, tm):
    aff = _sigmoid(logits_t)
    sel = aff + b_router
    rows = [sel[e:e + 1, :] for e in range(N_EXPERTS)]
    best_score, best_group = None, None
    for g in range(N_EXPERT_GROUPS):
        r = rows[g * EXPERTS_PER_GROUP:(g + 1) * EXPERTS_PER_GROUP]
        score = None
        for a in range(EXPERTS_PER_GROUP):
            for b in range(a + 1, EXPERTS_PER_GROUP):
                pair = r[a] + r[b]
                score = pair if score is None else jnp.maximum(score, pair)
        if g == 0:
            best_score, best_group = score, jnp.zeros_like(score, dtype=jnp.int32)
        else:
            better = score > best_score
            best_group = jnp.where(better, g, best_group)
            best_score = jnp.where(better, score, best_score)
    masked = [jnp.where(best_group == e // EXPERTS_PER_GROUP, rows[e], -jnp.inf)
              for e in range(N_EXPERTS)]

    def arg_top(vals):
        top, idx = vals[0], jnp.zeros_like(best_group)
        for e in range(1, N_EXPERTS):
            better = vals[e] > top
            idx = jnp.where(better, e, idx)
            top = jnp.where(better, vals[e], top)
        return idx

    e1 = arg_top(masked)
    e2 = arg_top([jnp.where(e1 == e, -jnp.inf, masked[e]) for e in range(N_EXPERTS)])
    expert_id = lax.broadcasted_iota(jnp.int32, (N_EXPERTS, tm), 0)
    pick1 = expert_id == e1
    pick2 = expert_id == e2
    a1 = jnp.sum(jnp.where(pick1, aff, 0.0), axis=0, keepdims=True)
    a2 = jnp.sum(jnp.where(pick2, aff, 0.0), axis=0, keepdims=True)
    denom = a1 + a2
    chosen = jnp.logical_or(pick1, pick2)
    t_row = lax.broadcasted_iota(jnp.int32, (tm, tm), 0)
    t_col = lax.broadcasted_iota(jnp.int32, (tm, tm), 1)
    earlier = (t_row < t_col).astype(BF16)
    rank = jnp.dot(chosen.astype(BF16), earlier, preferred_element_type=F32) + count_ref[...]
    r1 = jnp.sum(jnp.where(pick1, rank, 0.0), axis=0, keepdims=True)
    r2 = jnp.sum(jnp.where(pick2, rank, 0.0), axis=0, keepdims=True)
    count_ref[...] = count_ref[...] + jnp.sum(chosen.astype(F32), axis=1, keepdims=True)
    zero = jnp.zeros_like(a1)
    return jnp.concatenate([e1.astype(F32), e2.astype(F32), r1, r2, a1 / denom, a2 / denom,
                            zero, zero], axis=0)


def _outproj_body(m_ref, w_ref, x_ref, gm_ref, g_ref, sc_ref, sh_ref, wr_ref, br_ref,
                  xo_ref, hp_ref, rt_ref, cnt_ref, count_ref, *, tm):
    @pl.when(pl.program_id(0) == 0)
    def _():
        count_ref[...] = jnp.zeros_like(count_ref)

    x_new = x_ref[...] + gm_ref[...] * jnp.dot(m_ref[...], w_ref[...], preferred_element_type=F32)
    xo_ref[...] = x_new
    h = _modnorm(x_new, g_ref[...], sc_ref[...], sh_ref[...])
    hp_ref[...] = h
    logits_t = lax.dot_general(wr_ref[...], h, (((1,), (1,)), ((), ())),
                               precision=HIGHEST, preferred_element_type=F32)
    rt_ref[...] = _route(logits_t, br_ref[...], count_ref, tm)
    cnt_ref[...] = jnp.broadcast_to(count_ref[...], cnt_ref.shape)


def _outproj_route(merged, w_out, x, gate_m, g_ffn, scale_f, shift_f, w_router_t, b_router_col):
    t, d = x.shape
    tm = ROW_TILE
    vec = pl.BlockSpec((1, d), lambda i: (0, 0))
    return pl.pallas_call(
        functools.partial(_outproj_body, tm=tm),
        grid=(t // tm,),
        in_specs=[pl.BlockSpec((tm, d), lambda i: (i, 0)),
                  pl.BlockSpec((d, d), lambda i: (0, 0)),
                  pl.BlockSpec((tm, d), lambda i: (i, 0)),
                  vec, vec, vec, vec,
                  pl.BlockSpec((N_EXPERTS, d), lambda i: (0, 0)),
                  pl.BlockSpec((N_EXPERTS, 1), lambda i: (0, 0))],
        out_specs=[pl.BlockSpec((tm, d), lambda i: (i, 0)),
                   pl.BlockSpec((tm, d), lambda i: (i, 0)),
                   pl.BlockSpec((8, tm), lambda i: (0, i)),
                   pl.BlockSpec((N_EXPERTS, LANES), lambda i: (0, 0))],
        out_shape=[jax.ShapeDtypeStruct((t, d), F32),
                   jax.ShapeDtypeStruct((t, d), F32),
                   jax.ShapeDtypeStruct((8, t), F32),
                   jax.ShapeDtypeStruct((N_EXPERTS, LANES), F32)],
        scratch_shapes=[pltpu.VMEM((N_EXPERTS, 1), F32)],
        compiler_params=_params(("arbitrary",)),
        name="outproj_route",
    )(merged, w_out, x, gate_m, g_ffn, scale_f, shift_f, w_router_t, b_router_col)


def _dispatch_body(dest_ref, h_ref, init_ref, xs_ref, sem, *, tm, t):
    del init_ref
    base = pl.program_id(0) * tm

    def row_copy(r, d):
        return pltpu.make_async_copy(h_ref.at[pl.ds(r, 1)], xs_ref.at[pl.ds(d, 1)], sem)

    def issue(r, carry):
        row_copy(r, dest_ref[base + r]).start()
        row_copy(r, dest_ref[t + base + r]).start()
        return carry

    lax.fori_loop(0, tm, issue, 0)
    for _ in range(2):
        pltpu.make_async_copy(h_ref, xs_ref.at[pl.ds(0, tm)], sem).wait()


def _dispatch(dest, h, n_rows):
    t, d = h.shape
    tm = ROW_TILE
    init = jnp.zeros((n_rows, d), F32)
    return pl.pallas_call(
        functools.partial(_dispatch_body, tm=tm, t=t),
        grid_spec=pltpu.PrefetchScalarGridSpec(
            num_scalar_prefetch=1,
            grid=(t // tm,),
            in_specs=[pl.BlockSpec((tm, d), lambda i, dest: (i, 0)),
                      pl.BlockSpec(memory_space=pl.ANY)],
            out_specs=pl.BlockSpec(memory_space=pl.ANY),
            scratch_shapes=[pltpu.SemaphoreType.DMA(())]),
        out_shape=jax.ShapeDtypeStruct((n_rows, d), F32),
        input_output_aliases={2: 0},
        compiler_params=_params(("arbitrary",)),
        name="moe_dispatch",
    )(dest, h, init)


def _expert_body(te_ref, nu_ref, xs_ref, wg_ref, wu_ref, wd_ref, o_ref):
    del te_ref

    @pl.when(pl.program_id(0) < nu_ref[0])
    def _():
        xb = xs_ref[...].astype(BF16)
        gate = jnp.dot(xb, wg_ref[...], preferred_element_type=F32)
        up = jnp.dot(xb, wu_ref[...], preferred_element_type=F32)
        act = (gate * _sigmoid(gate) * up).astype(BF16)
        o_ref[...] = jnp.dot(act, wd_ref[...], preferred_element_type=F32)

    @pl.when(pl.program_id(0) >= nu_ref[0])
    def _():
        o_ref[...] = jnp.zeros_like(o_ref)


def _experts(tile_expert, n_used, xs, layer, w_gate, w_up, w_down):
    n_rows = xs.shape[0]
    te = EXPERT_TILE

    def row_map(i, tex, nu):
        return (jnp.maximum(jnp.minimum(i, nu[0] - 1), 0), 0)

    def w_map(i, tex, nu):
        return (layer, tex[i], 0, 0)

    return pl.pallas_call(
        _expert_body,
        grid_spec=pltpu.PrefetchScalarGridSpec(
            num_scalar_prefetch=2,
            grid=(n_rows // te,),
            in_specs=[pl.BlockSpec((te, D_MODEL), row_map),
                      pl.BlockSpec((None, None, D_MODEL, D_FF_EXPERT), w_map),
                      pl.BlockSpec((None, None, D_MODEL, D_FF_EXPERT), w_map),
                      pl.BlockSpec((None, None, D_FF_EXPERT, D_MODEL), w_map)],
            out_specs=pl.BlockSpec((te, D_MODEL), lambda i, tex, nu: (i, 0))),
        out_shape=jax.ShapeDtypeStruct((n_rows, D_MODEL), F32),
        compiler_params=_params(("arbitrary",)),
        name="moe_experts",
    )(tile_expert, n_used, xs, w_gate, w_up, w_down)


def _combine_body(dest_ref, ys_ref, x_ref, w_ref, gf_ref, o_ref, a_buf, b_buf, sem, *, tm, t):
    base = pl.program_id(0) * tm

    def row_copy(d, buf, r):
        return pltpu.make_async_copy(ys_ref.at[pl.ds(d, 1)], buf.at[pl.ds(r, 1)], sem)

    def issue(r, carry):
        row_copy(dest_ref[base + r], a_buf, r).start()
        row_copy(dest_ref[t + base + r], b_buf, r).start()
        return carry

    lax.fori_loop(0, tm, issue, 0)
    pltpu.make_async_copy(ys_ref.at[pl.ds(0, tm)], a_buf, sem).wait()
    pltpu.make_async_copy(ys_ref.at[pl.ds(0, tm)], b_buf, sem).wait()
    w = w_ref[...]
    y = w[:, 0:1] * a_buf[...] + w[:, 1:2] * b_buf[...]
    o_ref[...] = x_ref[...] + gf_ref[...] * y


def _combine(dest, ys, x, w_cols, gate_f):
    t, d = x.shape
    tm = COMBINE_TILE
    return pl.pallas_call(
        functools.partial(_combine_body, tm=tm, t=t),
        grid_spec=pltpu.PrefetchScalarGridSpec(
            num_scalar_prefetch=1,
            grid=(t // tm,),
            in_specs=[pl.BlockSpec(memory_space=pl.ANY),
                      pl.BlockSpec((tm, d), lambda i, dest: (i, 0)),
                      pl.BlockSpec((tm, 8), lambda i, dest: (i, 0)),
                      pl.BlockSpec((1, d), lambda i, dest: (0, 0))],
            out_specs=pl.BlockSpec((tm, d), lambda i, dest: (i, 0)),
            scratch_shapes=[pltpu.VMEM((tm, d), F32), pltpu.VMEM((tm, d), F32),
                            pltpu.SemaphoreType.DMA(())]),
        out_shape=jax.ShapeDtypeStruct((t, d), F32),
        compiler_params=_params(("arbitrary",)),
        name="moe_combine",
    )(dest, ys, x, w_cols, gate_f)


def _pad_lanes(v, fill=0.0):
    return jnp.pad(v.astype(F32), (0, LANES - v.shape[0]), constant_values=fill).reshape(1, LANES)


def _routing_tables(route, counts, t):
    te = EXPERT_TILE
    n_tiles = (2 * t) // te + N_EXPERTS
    cnt = counts[:, 0].astype(jnp.int32)
    padded = ((cnt + te - 1) // te) * te
    ends = jnp.cumsum(padded)
    offsets = ends - padded
    e1 = route[0].astype(jnp.int32)
    e2 = route[1].astype(jnp.int32)
    d1 = offsets[e1] + route[2].astype(jnp.int32)
    d2 = offsets[e2] + route[3].astype(jnp.int32)
    dest = jnp.concatenate([d1, d2]).astype(jnp.int32)
    n_used = (ends[-1] // te).astype(jnp.int32)
    tile_start = jnp.arange(n_tiles, dtype=jnp.int32) * te
    tile_expert = jnp.sum((tile_start[:, None] >= ends[None, :]).astype(jnp.int32), axis=1)
    last_expert = tile_expert[jnp.maximum(n_used - 1, 0)]
    tile_expert = jnp.where(jnp.arange(n_tiles) < n_used, tile_expert, last_expert)
    tile_expert = jnp.minimum(tile_expert, N_EXPERTS - 1).astype(jnp.int32)
    w_cols = jnp.transpose(route)
    w_cols = jnp.concatenate([w_cols[:, 4:6], w_cols[:, 0:6]], axis=1)
    return dest, tile_expert, n_used.reshape(1), w_cols, n_tiles * te


def kernel(x, c, w_ada, b_ada, g_norm_mix, w_in, b_fgate, g_q_fox, g_k_fox, conv_w, conv_b,
           dt_bias, a_log, d_skip, g_ssm_norm, w_branch_sb, w_branch_fox, w_branch_ssm, w_out,
           g_norm_ffn, w_router, b_router, w_e_gate, w_e_up, w_e_down):
    bsz, t, d = x.shape
    assert bsz == 1 and d == D_MODEL
    n_layers = w_ada.shape[0]
    xt = x.reshape(t, d)

    mod = _ada(c, w_ada, b_ada)
    expand = (jnp.arange(SSM_INNER)[None, :] // SSM_HEAD_DIM
              == jnp.arange(LANES)[:, None]).astype(F32)
    w_router_t = jnp.transpose(w_router)
    b_router_col = b_router.reshape(N_EXPERTS, 1)

    o_ff = 3 * SB_WIDTH + 3 * FOX_WIDTH
    o_z = o_ff + FOX_HEADS
    o_xbc = o_z + SSM_INNER
    o_dt = o_xbc + SSM_CONV_DIM
    o_gate = o_dt + SSM_HEADS
    width = w_in.shape[2]
    assert o_ff % IN_TILE == 0 and o_gate + N_BRANCH * d == width
    qkv_tiles = o_ff // IN_TILE
    mid_end = -(-o_gate // IN_TILE) * IN_TILE
    mid_tiles = (mid_end - o_ff) // IN_TILE
    full_end = width // IN_TILE * IN_TILE
    head_start = o_gate // LANES * LANES
    tail_pad = -(-(width - full_end) // LANES) * LANES
    q_scale = jnp.concatenate([jnp.full((SB_WIDTH,), HEAD_DIM ** -0.5 * LOG2E, F32),
                               jnp.ones((o_ff - SB_WIDTH,), F32)]).reshape(1, o_ff)
    w_gate_b, w_up_b, w_down_b = _cast_bf16(w_e_gate), _cast_bf16(w_e_up), _cast_bf16(w_e_down)
    w_tail = jnp.pad(lax.slice_in_dim(w_in, full_end, width, axis=2),
                     ((0, 0), (0, 0), (0, tail_pad - (width - full_end))))

    for layer in range(n_layers):
        m = mod[layer].reshape(6, 1, d)
        shift_m, scale_m, gate_m, shift_f, scale_f, gate_f = (m[i] for i in range(6))

        h = _norm(xt, g_norm_mix[layer].reshape(1, d), scale_m, shift_m)
        qkv = _proj(h, w_in, layer, 0, qkv_tiles, BF16, col_scale=q_scale, name="in_proj_qkv")
        mid = _proj(h, w_in, layer, qkv_tiles, mid_tiles, F32, name="in_proj_mid")
        gates = _proj(h, w_in, layer, mid_end // IN_TILE, (full_end - mid_end) // IN_TILE, BF16,
                      act="sigmoid", name="in_proj_gates")
        gates_tail = _mm(h, w_tail[layer], BF16, tail_pad, act="sigmoid", name="in_proj_gates_tail")

        o_sb = _sb_attention(qkv)
        qn, kn, f_col = _fox_prep(qkv, mid, _pad_lanes(b_fgate[layer]),
                                  g_q_fox[layer].reshape(1, HEAD_DIM),
                                  g_k_fox[layer].reshape(1, HEAD_DIM))
        f_row = jnp.transpose(f_col[:, :8])
        o_fox = _fox_attention(qn, kn, qkv, f_col, f_row)
        o_ssm = _ssd(mid, (o_z - o_ff, o_xbc - o_ff, o_dt - o_ff), conv_w[layer],
                     conv_b[layer].reshape(1, SSM_CONV_DIM),
                     _pad_lanes(dt_bias[layer]), _pad_lanes(-jnp.exp(a_log[layer].astype(F32))),
                     jnp.repeat(d_skip[layer].astype(F32), SSM_HEAD_DIM).reshape(1, SSM_INNER),
                     g_ssm_norm[layer].reshape(1, SSM_INNER), expand)

        merged = _merge(mid, gates, gates_tail, mid_end - head_start, o_gate - head_start,
                        o_sb, o_fox, o_ssm, w_branch_sb[layer].astype(BF16),
                        w_branch_fox[layer].astype(BF16), w_branch_ssm[layer].astype(BF16))
        xt, h_moe, route, counts = _outproj_route(
            merged, w_out[layer].astype(BF16), xt, gate_m, g_norm_ffn[layer].reshape(1, d),
            scale_f, shift_f, w_router_t, b_router_col)

        dest, tile_expert, n_used, w_cols, n_rows = _routing_tables(route, counts, t)
        xs = _dispatch(dest, h_moe, n_rows)
        ys = _experts(tile_expert, n_used, xs, layer, w_gate_b, w_up_b, w_down_b)
        xt = _combine(dest, ys, xt, w_cols, gate_f)
    return xt.reshape(bsz, t, d)
```

```python
import functools

import jax
import jax.numpy as jnp
from jax import lax
from jax.experimental import pallas as pl
from jax.experimental.pallas import tpu as pltpu

F32 = jnp.float32
BF16 = jnp.bfloat16
HIGHEST = lax.Precision.HIGHEST
LOG2E = 1.4426950408889634

D_MODEL = 2048
EPS = 1e-6
HEAD_DIM = 128
SB_HEADS = 4
FOX_HEADS = 4
SB_WIDTH = SB_HEADS * HEAD_DIM
FOX_WIDTH = FOX_HEADS * HEAD_DIM
SSM_HEAD_DIM = 64
SSM_HEADS = 16
SSM_INNER = SSM_HEADS * SSM_HEAD_DIM
SSM_GROUPS = 2
SSM_STATE = 128
CONV_WIDTH = 4
SSM_CONV_DIM = SSM_INNER + 2 * SSM_GROUPS * SSM_STATE
N_BRANCH = 3
N_EXPERTS = 16
N_EXPERT_GROUPS = 4
EXPERTS_PER_GROUP = N_EXPERTS // N_EXPERT_GROUPS
D_FF_EXPERT = 1024

LANES = 128
V7X_VMEM_BYTES = 64 * 1024 * 1024
VMEM_LIMIT = 48 * 1024 * 1024

ROW_TILE = 512
IN_TILE = 1024
ATTN_TILE = 256
ATTN_SUB = 4
KEY_UNROLL = 4
SSD_CHUNK = 128
EXPERT_TILE = 256
COMBINE_TILE = 512


def _params(sem, vmem=VMEM_LIMIT):
    return pltpu.CompilerParams(dimension_semantics=sem, vmem_limit_bytes=vmem)


def _sigmoid(x):
    return 1.0 / (1.0 + jnp.exp(-x))


def _log_sigmoid(x):
    return jnp.minimum(x, 0.0) - jnp.log(1.0 + jnp.exp(-jnp.abs(x)))


def _softplus(x):
    return jnp.maximum(x, 0.0) + jnp.log(1.0 + jnp.exp(-jnp.abs(x)))


def _ada_body(c_ref, w_ref, b_ref, o_ref):
    k = pl.program_id(1)

    @pl.when(k == 0)
    def _():
        o_ref[0] = b_ref[0]

    c = c_ref[...]
    cond = c * _sigmoid(c)
    parts = [jnp.sum(w_ref[0, :, j * LANES:(j + 1) * LANES] * cond, axis=0, keepdims=True)
             for j in range(w_ref.shape[2] // LANES)]
    o_ref[0] += jnp.concatenate(parts, axis=1)


def _ada(c, w_ada, b_ada):
    n_layers, d, n = w_ada.shape
    tk = 256
    c_lanes = jnp.broadcast_to(c.reshape(d, 1), (d, LANES))
    out = pl.pallas_call(
        _ada_body,
        grid=(n_layers, d // tk),
        in_specs=[pl.BlockSpec((tk, LANES), lambda l, k: (k, 0)),
                  pl.BlockSpec((1, tk, n), lambda l, k: (l, k, 0)),
                  pl.BlockSpec((1, 1, n), lambda l, k: (l, 0, 0))],
        out_specs=pl.BlockSpec((1, 1, n), lambda l, k: (l, 0, 0)),
        out_shape=jax.ShapeDtypeStruct((n_layers, 1, n), F32),
        compiler_params=_params(("arbitrary", "arbitrary")),
        name="ada_mod",
    )(c_lanes, w_ada, b_ada.reshape(n_layers, 1, n))
    return out[:, 0, :]


def _modnorm(x, g, scale, shift):
    y = x * lax.rsqrt(jnp.mean(x * x, axis=-1, keepdims=True) + EPS) * g
    return y * (1.0 + scale) + shift


def _norm_body(x_ref, g_ref, sc_ref, sh_ref, o_ref):
    o_ref[...] = _modnorm(x_ref[...], g_ref[...], sc_ref[...], sh_ref[...]).astype(BF16)


def _norm(x, g, scale, shift):
    t, d = x.shape
    vec = pl.BlockSpec((1, d), lambda i: (0, 0))
    return pl.pallas_call(
        _norm_body,
        grid=(t // ROW_TILE,),
        in_specs=[pl.BlockSpec((ROW_TILE, d), lambda i: (i, 0)), vec, vec, vec],
        out_specs=pl.BlockSpec((ROW_TILE, d), lambda i: (i, 0)),
        out_shape=jax.ShapeDtypeStruct((t, d), BF16),
        compiler_params=_params(("arbitrary",)),
        name="mod_norm",
    )(x, g, scale, shift)


def _mm_body(a_ref, w_ref, o_ref, *, act):
    r = jnp.dot(a_ref[...], w_ref[...].astype(BF16), preferred_element_type=F32)
    if act == "sigmoid":
        r = _sigmoid(r)
    o_ref[...] = r.astype(o_ref.dtype)


def _mm(a, w, out_dtype, tn, act=None, tm=1024, name="matmul"):
    m, k = a.shape
    n = w.shape[1]
    tm = min(tm, m)
    return pl.pallas_call(
        functools.partial(_mm_body, act=act),
        grid=(m // tm, n // tn),
        in_specs=[pl.BlockSpec((tm, k), lambda i, j: (i, 0)),
                  pl.BlockSpec((k, tn), lambda i, j: (0, j))],
        out_specs=pl.BlockSpec((tm, tn), lambda i, j: (i, j)),
        out_shape=jax.ShapeDtypeStruct((m, n), out_dtype),
        compiler_params=_params(("arbitrary", "arbitrary")),
        name=name,
    )(a, w)


def _cast_body(w_ref, o_ref):
    o_ref[...] = w_ref[...].astype(o_ref.dtype)


def _cast_bf16(w):
    shape = w.shape
    n = shape[-1]
    rows = w.size // n
    tr = min(rows, (2 * 1024 * 1024) // n)
    out = pl.pallas_call(
        _cast_body,
        grid=(rows // tr,),
        in_specs=[pl.BlockSpec((tr, n), lambda i: (i, 0))],
        out_specs=pl.BlockSpec((tr, n), lambda i: (i, 0)),
        out_shape=jax.ShapeDtypeStruct((rows, n), BF16),
        compiler_params=_params(("arbitrary",)),
        name="cast_bf16",
    )(w.reshape(rows, n))
    return out.reshape(shape)


def _proj_body(*refs, act, scaled):
    if scaled:
        h_ref, w_ref, sc_ref, o_ref, wb_ref = refs
    else:
        h_ref, w_ref, o_ref, wb_ref = refs

    @pl.when(pl.program_id(1) == 0)
    def _():
        wb_ref[...] = w_ref[0].astype(BF16)

    r = jnp.dot(h_ref[...], wb_ref[...], preferred_element_type=F32)
    if scaled:
        r = r * sc_ref[...]
    if act == "sigmoid":
        r = _sigmoid(r)
    o_ref[...] = r.astype(o_ref.dtype)


def _proj(h, w_in, layer, first_tile, n_tiles, out_dtype, act=None, col_scale=None, name="in_proj"):
    t, k = h.shape
    tm, tn = 1024, IN_TILE
    assert (first_tile + n_tiles) * tn <= w_in.shape[2]
    in_specs = [pl.BlockSpec((tm, k), lambda j, i: (i, 0)),
                pl.BlockSpec((1, k, tn), lambda j, i: (layer, 0, first_tile + j))]
    args = [h, w_in]
    if col_scale is not None:
        in_specs.append(pl.BlockSpec((1, tn), lambda j, i: (0, j)))
        args.append(col_scale)
    return pl.pallas_call(
        functools.partial(_proj_body, act=act, scaled=col_scale is not None),
        grid=(n_tiles, t // tm),
        in_specs=in_specs,
        out_specs=pl.BlockSpec((tm, tn), lambda j, i: (i, j)),
        out_shape=jax.ShapeDtypeStruct((t, n_tiles * tn), out_dtype),
        scratch_shapes=[pltpu.VMEM((k, tn), BF16)],
        compiler_params=_params(("arbitrary", "arbitrary")),
        name=name,
    )(*args)


def _fox_prep_body(q_ref, k_ref, f_ref, bf_ref, gq_ref, gk_ref,
                   qo_ref, ko_ref, fo_ref, carry_ref, *, tm):
    @pl.when(pl.program_id(0) == 0)
    def _():
        carry_ref[...] = jnp.zeros_like(carry_ref)

    scale = HEAD_DIM ** -0.5 * LOG2E
    for h in range(FOX_HEADS):
        sl = slice(h * HEAD_DIM, (h + 1) * HEAD_DIM)
        q = q_ref[:, sl].astype(F32)
        qn = q * lax.rsqrt(jnp.mean(q * q, axis=-1, keepdims=True) + EPS) * gq_ref[...]
        qo_ref[:, sl] = (qn * scale).astype(BF16)
        k = k_ref[:, sl].astype(F32)
        kn = k * lax.rsqrt(jnp.mean(k * k, axis=-1, keepdims=True) + EPS) * gk_ref[...]
        ko_ref[:, sl] = kn.astype(BF16)

    log_f = _log_sigmoid(f_ref[...] + bf_ref[...])
    row = lax.broadcasted_iota(jnp.int32, (tm, tm), 0)
    col = lax.broadcasted_iota(jnp.int32, (tm, tm), 1)
    tri = (col <= row).astype(F32)
    cum = jnp.dot(tri, log_f, precision=HIGHEST, preferred_element_type=F32) + carry_ref[...]
    fo_ref[...] = cum * LOG2E
    carry_ref[...] = cum[tm - 1:tm, :]


def _fox_prep(qkv, misc, b_fgate_pad, g_q, g_k):
    t = qkv.shape[0]
    tm = ROW_TILE
    q_blk = 3 * SB_WIDTH // FOX_WIDTH
    vec = pl.BlockSpec((1, LANES), lambda i: (0, 0))
    return pl.pallas_call(
        functools.partial(_fox_prep_body, tm=tm),
        grid=(t // tm,),
        in_specs=[pl.BlockSpec((tm, FOX_WIDTH), lambda i: (i, q_blk)),
                  pl.BlockSpec((tm, FOX_WIDTH), lambda i: (i, q_blk + 1)),
                  pl.BlockSpec((tm, LANES), lambda i: (i, 0)),
                  vec, vec, vec],
        out_specs=[pl.BlockSpec((tm, FOX_WIDTH), lambda i: (i, 0)),
                   pl.BlockSpec((tm, FOX_WIDTH), lambda i: (i, 0)),
                   pl.BlockSpec((tm, LANES), lambda i: (i, 0))],
        out_shape=[jax.ShapeDtypeStruct((t, FOX_WIDTH), BF16),
                   jax.ShapeDtypeStruct((t, FOX_WIDTH), BF16),
                   jax.ShapeDtypeStruct((t, LANES), F32)],
        scratch_shapes=[pltpu.VMEM((1, LANES), F32)],
        compiler_params=_params(("arbitrary",)),
        name="fox_prep",
    )(qkv, qkv, misc, b_fgate_pad, g_q, g_k)


def _sb_body(q_ref, k_ref, v_ref, o_ref, *, tb, nsub):
    qb = pl.program_id(1)
    qs = [q_ref[a * tb:(a + 1) * tb, :] for a in range(nsub)]
    row = lax.broadcasted_iota(jnp.int32, (tb, tb), 0)
    col = lax.broadcasted_iota(jnp.int32, (tb, tb), 1)
    ones = (row >= col).astype(BF16)
    suffix_ones = jnp.concatenate([ones, ones], axis=0)
    past = col < row

    def load(kb):
        ks = pl.multiple_of(kb * tb, tb)
        return k_ref[pl.ds(ks, tb), :], v_ref[pl.ds(ks, tb), :]

    def sweep(k, v, chains):
        us = [lax.dot_general(q, k, (((1,), (1,)), ((), ())), preferred_element_type=F32)
              for q, _, _, _ in chains]
        splits = []
        for u, (_, _, _, diagonal) in zip(us, chains):
            drop = jnp.maximum(u, 0.0) + jnp.log2(1.0 + jnp.exp2(-jnp.abs(u)))
            if diagonal:
                drop = jnp.where(past, drop, 0.0)
            hi = drop.astype(BF16)
            lo = (drop - hi.astype(F32)).astype(BF16)
            splits.append(jnp.concatenate([hi, lo], axis=1))
        sufs = [jnp.dot(s, suffix_ones, preferred_element_type=F32) for s in splits]
        ws = []
        for u, suf, (_, carry, _, diagonal) in zip(us, sufs, chains):
            w = jnp.exp2(u - suf - carry)
            if diagonal:
                w = jnp.where(past, w, 0.0)
            ws.append(w.astype(BF16))
        return [(carry + suf[:, 0:1], acc + jnp.dot(w, v, preferred_element_type=F32))
                for w, suf, (_, carry, acc, _) in zip(ws, sufs, chains)]

    state = [(jnp.zeros((tb, 1), F32), jnp.zeros((tb, HEAD_DIM), F32)) for _ in range(nsub)]
    for j in reversed(range(nsub)):
        k, v = load(qb * nsub + j)
        new = sweep(k, v, [(qs[a], state[a][0], state[a][1], a == j) for a in range(j, nsub)])
        state[j:] = new

    def step(i, c):
        for j in range(KEY_UNROLL):
            k, v = load(qb * nsub - 1 - (i * KEY_UNROLL + j))
            c = tuple(sweep(k, v, [(qs[a], c[a][0], c[a][1], False) for a in range(nsub)]))
        return c

    state = lax.fori_loop(0, qb * nsub // KEY_UNROLL, step, tuple(state))
    for a in range(nsub):
        o_ref[a * tb:(a + 1) * tb, :] = state[a][1].astype(o_ref.dtype)


def _sb_attention(qkv):
    t = qkv.shape[0]
    tb, nsub = ATTN_TILE, ATTN_SUB
    return pl.pallas_call(
        functools.partial(_sb_body, tb=tb, nsub=nsub),
        grid=(SB_HEADS, t // (tb * nsub)),
        in_specs=[pl.BlockSpec((tb * nsub, HEAD_DIM), lambda h, i: (i, h)),
                  pl.BlockSpec((t, HEAD_DIM), lambda h, i: (0, SB_HEADS + h)),
                  pl.BlockSpec((t, HEAD_DIM), lambda h, i: (0, 2 * SB_HEADS + h))],
        out_specs=pl.BlockSpec((tb * nsub, HEAD_DIM), lambda h, i: (i, h)),
        out_shape=jax.ShapeDtypeStruct((t, SB_WIDTH), BF16),
        compiler_params=_params(("arbitrary", "arbitrary")),
        name="sb_attention",
    )(qkv, qkv, qkv)


def _fox_body(q_ref, k_ref, v_ref, fcol_ref, frow_ref, o_ref, *, tb, nsub):
    h = pl.program_id(0)
    qb = pl.program_id(1)
    lane = lax.broadcasted_iota(jnp.int32, (tb, LANES), 1)
    qs, f_qs = [], []
    for a in range(nsub):
        qs.append(q_ref[a * tb:(a + 1) * tb, :])
        f_qs.append(jnp.sum(jnp.where(lane == h, fcol_ref[a * tb:(a + 1) * tb, :], 0.0),
                            axis=-1, keepdims=True))
    row = lax.broadcasted_iota(jnp.int32, (tb, tb), 0)
    col = lax.broadcasted_iota(jnp.int32, (tb, tb), 1)
    causal = col <= row

    def load(kb):
        ks = pl.multiple_of(kb * tb, tb)
        return (k_ref[pl.ds(ks, tb), :], v_ref[pl.ds(ks, tb), :],
                frow_ref[pl.ds(h, 1), pl.ds(ks, tb)])

    def sweep(k, v, f_k, chains):
        ss = []
        for a, _, diagonal in chains:
            s = lax.dot_general(qs[a], k, (((1,), (1,)), ((), ())), preferred_element_type=F32) - f_k
            ss.append(jnp.where(causal, s, -jnp.inf) if diagonal else s)
        ms, ps, sums = [], [], []
        for s, (a, (m, _, _), _) in zip(ss, chains):
            m_new = jnp.maximum(m, jnp.max(s, axis=-1, keepdims=True) + f_qs[a])
            p = jnp.exp2(s - (m_new - f_qs[a]))
            ms.append(m_new)
            sums.append(jnp.sum(p, axis=-1, keepdims=True))
            ps.append(p.astype(BF16))
        out = []
        for p, p_sum, m_new, (_, (m, l, acc), _) in zip(ps, sums, ms, chains):
            alpha = jnp.exp2(m - m_new)
            out.append((m_new, alpha * l + p_sum,
                        alpha * acc + jnp.dot(p, v, preferred_element_type=F32)))
        return out

    def step(i, c):
        for j in range(KEY_UNROLL):
            k, v, f_k = load(i * KEY_UNROLL + j)
            c = tuple(sweep(k, v, f_k, [(a, c[a], False) for a in range(nsub)]))
        return c

    init = tuple((jnp.full((tb, 1), -jnp.inf, F32), jnp.zeros((tb, 1), F32),
                  jnp.zeros((tb, HEAD_DIM), F32)) for _ in range(nsub))
    state = list(lax.fori_loop(0, qb * nsub // KEY_UNROLL, step, init))
    for j in range(nsub):
        k, v, f_k = load(qb * nsub + j)
        state[j:] = sweep(k, v, f_k, [(a, state[a], a == j) for a in range(j, nsub)])
    for a in range(nsub):
        o_ref[a * tb:(a + 1) * tb, :] = (state[a][2] / state[a][1]).astype(o_ref.dtype)


def _fox_attention(qn, kn, qkv, f_col, f_row):
    t = qn.shape[0]
    tb, nsub = ATTN_TILE, ATTN_SUB
    v_blk = (3 * SB_WIDTH + 2 * FOX_WIDTH) // HEAD_DIM
    return pl.pallas_call(
        functools.partial(_fox_body, tb=tb, nsub=nsub),
        grid=(FOX_HEADS, t // (tb * nsub)),
        in_specs=[pl.BlockSpec((tb * nsub, HEAD_DIM), lambda h, i: (i, h)),
                  pl.BlockSpec((t, HEAD_DIM), lambda h, i: (0, h)),
                  pl.BlockSpec((t, HEAD_DIM), lambda h, i: (0, v_blk + h)),
                  pl.BlockSpec((tb * nsub, LANES), lambda h, i: (i, 0)),
                  pl.BlockSpec((8, t), lambda h, i: (0, 0))],
        out_specs=pl.BlockSpec((tb * nsub, HEAD_DIM), lambda h, i: (i, h)),
        out_shape=jax.ShapeDtypeStruct((t, FOX_WIDTH), BF16),
        compiler_params=_params(("arbitrary", "arbitrary")),
        name="fox_attention",
    )(qn, kn, qkv, f_col, f_row)


def _ssd_body(mid_ref, cw_ref, cb_ref, dtb_ref, a_ref, dsk_ref, gn_ref, ex_ref,
              o_ref, ext_ref, state_ref, *, cl, off_z, off_xbc, off_dt):
    hg = SSM_HEADS // SSM_GROUPS
    gw = hg * SSM_HEAD_DIM

    @pl.when(pl.program_id(0) == 0)
    def _():
        ext_ref[0:8, :] = jnp.zeros((8, SSM_CONV_DIM), F32)
        state_ref[...] = jnp.zeros_like(state_ref)

    z = mid_ref[:, off_z:off_z + SSM_INNER]
    xbc_raw = mid_ref[:, off_xbc:off_xbc + SSM_CONV_DIM]
    lane = lax.broadcasted_iota(jnp.int32, (cl, LANES), 1)
    dt_raw = jnp.where(lane < SSM_HEADS, mid_ref[:, off_dt:off_dt + LANES], 0.0)

    ext_ref[8:8 + cl, :] = xbc_raw
    conv = cb_ref[...] + cw_ref[0:1, :] * ext_ref[pl.ds(8 - (CONV_WIDTH - 1), cl), :]
    for w in range(1, CONV_WIDTH):
        conv = conv + cw_ref[w:w + 1, :] * ext_ref[pl.ds(8 - (CONV_WIDTH - 1) + w, cl), :]
    ext_ref[0:8, :] = xbc_raw[cl - 8:cl, :]
    xbc = conv * _sigmoid(conv)
    xs = xbc[:, :SSM_INNER]
    b_mat = xbc[:, SSM_INNER:SSM_INNER + SSM_GROUPS * SSM_STATE]
    c_mat = xbc[:, SSM_INNER + SSM_GROUPS * SSM_STATE:]

    dt = _softplus(dt_raw + dtb_ref[...])
    log_a = dt * a_ref[...]
    row = lax.broadcasted_iota(jnp.int32, (cl, cl), 0)
    col = lax.broadcasted_iota(jnp.int32, (cl, cl), 1)
    causal = col <= row
    a_cum = jnp.dot(causal.astype(F32), log_a, precision=HIGHEST, preferred_element_type=F32)
    a_cum_t = a_cum.T
    expand = ex_ref[...]

    def widen(v):
        return jnp.dot(v, expand, precision=HIGHEST, preferred_element_type=F32)

    dt_w = widen(dt)
    a_cum_w = widen(a_cum)
    a_last_w = a_cum_w[cl - 1:cl, :]
    to_end_w = jnp.exp(a_last_w - a_cum_w)
    from_start_w = jnp.exp(a_cum_w)
    chunk_decay_w = jnp.exp(a_last_w)
    xc = xs * dt_w
    xw = (xc * to_end_w).astype(BF16)
    xc_b = xc.astype(BF16)
    low_half = lane < SSM_HEAD_DIM

    y_parts = []
    for g in range(SSM_GROUPS):
        bg = b_mat[:, g * SSM_STATE:(g + 1) * SSM_STATE].astype(BF16)
        cg = c_mat[:, g * SSM_STATE:(g + 1) * SSM_STATE].astype(BF16)
        cb = lax.dot_general(cg, bg, (((1,), (1,)), ((), ())), preferred_element_type=F32)
        state = state_ref[g]
        y_off = jnp.dot(cg, state.astype(BF16), preferred_element_type=F32)
        y_off = y_off * from_start_w[:, g * gw:(g + 1) * gw]
        new_state = lax.dot_general(bg, xw[:, g * gw:(g + 1) * gw], (((0,), (0,)), ((), ())),
                                    preferred_element_type=F32)
        state_ref[g] = state * chunk_decay_w[:, g * gw:(g + 1) * gw] + new_state
        for pair in range(hg // 2):
            h0 = g * hg + 2 * pair
            c0 = h0 * SSM_HEAD_DIM
            x_pair = xc_b[:, c0:c0 + LANES]
            y_pair = None
            for k in range(2):
                h = h0 + k
                seg = a_cum[:, h:h + 1] - a_cum_t[h:h + 1, :]
                decay = jnp.where(causal, jnp.exp(jnp.where(causal, seg, 0.0)), 0.0)
                m = (cb * decay).astype(BF16)
                x_half = jnp.where(low_half if k == 0 else jnp.logical_not(low_half),
                                   x_pair, jnp.zeros_like(x_pair))
                part = jnp.dot(m, x_half, preferred_element_type=F32)
                y_pair = part if y_pair is None else y_pair + part
            y_parts.append(y_pair + y_off[:, c0 - g * gw:c0 - g * gw + LANES])
    y = jnp.concatenate(y_parts, axis=-1) + dsk_ref[...] * xs
    yz = y * (z * _sigmoid(z))
    out = yz * lax.rsqrt(jnp.mean(yz * yz, axis=-1, keepdims=True) + EPS) * gn_ref[...]
    o_ref[...] = out.astype(o_ref.dtype)


def _ssd(mid, offsets, conv_w, conv_b, dt_bias_pad, a_pad, d_skip_w, g_norm, expand):
    t, width = mid.shape
    cl = SSD_CHUNK
    off_z, off_xbc, off_dt = offsets

    def full(shape):
        return pl.BlockSpec(shape, lambda i: (0,) * len(shape))

    return pl.pallas_call(
        functools.partial(_ssd_body, cl=cl, off_z=off_z, off_xbc=off_xbc, off_dt=off_dt),
        grid=(t // cl,),
        in_specs=[pl.BlockSpec((cl, width), lambda i: (i, 0)),
                  full((CONV_WIDTH, SSM_CONV_DIM)), full((1, SSM_CONV_DIM)),
                  full((1, LANES)), full((1, LANES)), full((1, SSM_INNER)),
                  full((1, SSM_INNER)), full((LANES, SSM_INNER))],
        out_specs=pl.BlockSpec((cl, SSM_INNER), lambda i: (i, 0)),
        out_shape=jax.ShapeDtypeStruct((t, SSM_INNER), BF16),
        scratch_shapes=[pltpu.VMEM((cl + 8, SSM_CONV_DIM), F32),
                        pltpu.VMEM((SSM_GROUPS, SSM_STATE, SSM_INNER // SSM_GROUPS), F32)],
        compiler_params=_params(("arbitrary",)),
        name="ssd_mixer",
    )(mid, conv_w, conv_b, dt_bias_pad, a_pad, d_skip_w, g_norm, expand)


def _merge_body(mid_ref, g_ref, gt_ref, osb_ref, ofx_ref, osm_ref, wsb_ref, wfx_ref, wsm_ref, o_ref,
                *, head_cols, gate_off, tn):
    head = _sigmoid(mid_ref[:, IN_TILE - head_cols:])
    gates = jnp.concatenate([head, g_ref[...].astype(F32), gt_ref[...].astype(F32)], axis=1)
    branches = ((osb_ref[...], wsb_ref), (ofx_ref[...], wfx_ref), (osm_ref[...], wsm_ref))
    for c in range(D_MODEL // tn):
        merged = None
        for b, (act, w_ref) in enumerate(branches):
            start = gate_off + b * D_MODEL + c * tn
            term = gates[:, start:start + tn] * jnp.dot(act, w_ref[:, c * tn:(c + 1) * tn],
                                                        preferred_element_type=F32)
            merged = term if merged is None else merged + term
        o_ref[:, c * tn:(c + 1) * tn] = merged.astype(o_ref.dtype)


def _merge(mid, gates, gates_tail, head_cols, gate_off, o_sb, o_fox, o_ssm, w_sb, w_fox, w_ssm):
    t = mid.shape[0]
    tm = 256

    def rows(width, col_block=0):
        return pl.BlockSpec((tm, width), lambda i: (i, col_block))

    def whole(shape):
        return pl.BlockSpec(shape, lambda i: (0, 0))

    return pl.pallas_call(
        functools.partial(_merge_body, head_cols=head_cols, gate_off=gate_off, tn=512),
        grid=(t // tm,),
        in_specs=[rows(IN_TILE, mid.shape[1] // IN_TILE - 1), rows(gates.shape[1]),
                  rows(gates_tail.shape[1]),
                  rows(SB_WIDTH), rows(FOX_WIDTH), rows(SSM_INNER),
                  whole(w_sb.shape), whole(w_fox.shape), whole(w_ssm.shape)],
        out_specs=rows(D_MODEL),
        out_shape=jax.ShapeDtypeStruct((t, D_MODEL), BF16),
        compiler_params=_params(("arbitrary",)),
        name="branch_merge",
    )(mid, gates, gates_tail, o_sb, o_fox, o_ssm, w_sb, w_fox, w_ssm)


def _route(logits_t, b_router, count_ref, tm):
    aff = _sigmoid(logits_t)
    sel = aff + b_router
    rows = [sel[e:e + 1, :] for e in range(N_EXPERTS)]
    best_score, best_group = None, None
    for g in range(N_EXPERT_GROUPS):
        r = rows[g * EXPERTS_PER_GROUP:(g + 1) * EXPERTS_PER_GROUP]
        score = None
        for a in range(EXPERTS_PER_GROUP):
            for b in range(a + 1, EXPERTS_PER_GROUP):
                pair = r[a] + r[b]
                score = pair if score is None else jnp.maximum(score, pair)
        if g == 0:
            best_score, best_group = score, jnp.zeros_like(score, dtype=jnp.int32)
        else:
            better = score > best_score
            best_group = jnp.where(better, g, best_group)
            best_score = jnp.where(better, score, best_score)
    masked = [jnp.where(best_group == e // EXPERTS_PER_GROUP, rows[e], -jnp.inf)
              for e in range(N_EXPERTS)]

    def arg_top(vals):
        top, idx = vals[0], jnp.zeros_like(best_group)
        for e in range(1, N_EXPERTS):
            better = vals[e] > top
            idx = jnp.where(better, e, idx)
            top = jnp.where(better, vals[e], top)
        return idx

    e1 = arg_top(masked)
    e2 = arg_top([jnp.where(e1 == e, -jnp.inf, masked[e]) for e in range(N_EXPERTS)])
    expert_id = lax.broadcasted_iota(jnp.int32, (N_EXPERTS, tm), 0)
    pick1 = expert_id == e1
    pick2 = expert_id == e2
    a1 = jnp.sum(jnp.where(pick1, aff, 0.0), axis=0, keepdims=True)
    a2 = jnp.sum(jnp.where(pick2, aff, 0.0), axis=0, keepdims=True)
    denom = a1 + a2
    chosen = jnp.logical_or(pick1, pick2)
    t_row = lax.broadcasted_iota(jnp.int32, (tm, tm), 0)
    t_col = lax.broadcasted_iota(jnp.int32, (tm, tm), 1)
    earlier = (t_row < t_col).astype(BF16)
    rank = jnp.dot(chosen.astype(BF16), earlier, preferred_element_type=F32) + count_ref[...]
    r1 = jnp.sum(jnp.where(pick1, rank, 0.0), axis=0, keepdims=True)
    r2 = jnp.sum(jnp.where(pick2, rank, 0.0), axis=0, keepdims=True)
    count_ref[...] = count_ref[...] + jnp.sum(chosen.astype(F32), axis=1, keepdims=True)
    zero = jnp.zeros_like(a1)
    return jnp.concatenate([e1.astype(F32), e2.astype(F32), r1, r2, a1 / denom, a2 / denom,
                            zero, zero], axis=0)


def _outproj_body(m_ref, w_ref, x_ref, gm_ref, g_ref, sc_ref, sh_ref, wr_ref, br_ref,
                  xo_ref, hp_ref, rt_ref, cnt_ref, count_ref, *, tm):
    @pl.when(pl.program_id(0) == 0)
    def _():
        count_ref[...] = jnp.zeros_like(count_ref)

    x_new = x_ref[...] + gm_ref[...] * jnp.dot(m_ref[...], w_ref[...], preferred_element_type=F32)
    xo_ref[...] = x_new
    h = _modnorm(x_new, g_ref[...], sc_ref[...], sh_ref[...])
    hp_ref[...] = h
    logits_t = lax.dot_general(wr_ref[...], h, (((1,), (1,)), ((), ())),
                               precision=HIGHEST, preferred_element_type=F32)
    rt_ref[...] = _route(logits_t, br_ref[...], count_ref, tm)
    cnt_ref[...] = jnp.broadcast_to(count_ref[...], cnt_ref.shape)


def _outproj_route(merged, w_out, x, gate_m, g_ffn, scale_f, shift_f, w_router_t, b_router_col):
    t, d = x.shape
    tm = ROW_TILE
    vec = pl.BlockSpec((1, d), lambda i: (0, 0))
    return pl.pallas_call(
        functools.partial(_outproj_body, tm=tm),
        grid=(t // tm,),
        in_specs=[pl.BlockSpec((tm, d), lambda i: (i, 0)),
                  pl.BlockSpec((d, d), lambda i: (0, 0)),
                  pl.BlockSpec((tm, d), lambda i: (i, 0)),
                  vec, vec, vec, vec,
                  pl.BlockSpec((N_EXPERTS, d), lambda i: (0, 0)),
                  pl.BlockSpec((N_EXPERTS, 1), lambda i: (0, 0))],
        out_specs=[pl.BlockSpec((tm, d), lambda i: (i, 0)),
                   pl.BlockSpec((tm, d), lambda i: (i, 0)),
                   pl.BlockSpec((8, tm), lambda i: (0, i)),
                   pl.BlockSpec((N_EXPERTS, LANES), lambda i: (0, 0))],
        out_shape=[jax.ShapeDtypeStruct((t, d), F32),
                   jax.ShapeDtypeStruct((t, d), F32),
                   jax.ShapeDtypeStruct((8, t), F32),
                   jax.ShapeDtypeStruct((N_EXPERTS, LANES), F32)],
        scratch_shapes=[pltpu.VMEM((N_EXPERTS, 1), F32)],
        compiler_params=_params(("arbitrary",)),
        name="outproj_route",
    )(merged, w_out, x, gate_m, g_ffn, scale_f, shift_f, w_router_t, b_router_col)


def _dispatch_body(dest_ref, h_ref, init_ref, xs_ref, sem, *, tm, t):
    del init_ref
    base = pl.program_id(0) * tm

    def row_copy(r, d):
        return pltpu.make_async_copy(h_ref.at[pl.ds(r, 1)], xs_ref.at[pl.ds(d, 1)], sem)

    def issue(r, carry):
        row_copy(r, dest_ref[base + r]).start()
        row_copy(r, dest_ref[t + base + r]).start()
        return carry

    lax.fori_loop(0, tm, issue, 0)
    for _ in range(2):
        pltpu.make_async_copy(h_ref, xs_ref.at[pl.ds(0, tm)], sem).wait()


def _dispatch(dest, h, n_rows):
    t, d = h.shape
    tm = ROW_TILE
    init = jnp.zeros((n_rows, d), F32)
    return pl.pallas_call(
        functools.partial(_dispatch_body, tm=tm, t=t),
        grid_spec=pltpu.PrefetchScalarGridSpec(
            num_scalar_prefetch=1,
            grid=(t // tm,),
            in_specs=[pl.BlockSpec((tm, d), lambda i, dest: (i, 0)),
                      pl.BlockSpec(memory_space=pl.ANY)],
            out_specs=pl.BlockSpec(memory_space=pl.ANY),
            scratch_shapes=[pltpu.SemaphoreType.DMA(())]),
        out_shape=jax.ShapeDtypeStruct((n_rows, d), F32),
        input_output_aliases={2: 0},
        compiler_params=_params(("arbitrary",)),
        name="moe_dispatch",
    )(dest, h, init)


def _expert_body(te_ref, nu_ref, xs_ref, wg_ref, wu_ref, wd_ref, o_ref):
    del te_ref

    @pl.when(pl.program_id(0) < nu_ref[0])
    def _():
        xb = xs_ref[...].astype(BF16)
        gate = jnp.dot(xb, wg_ref[...], preferred_element_type=F32)
        up = jnp.dot(xb, wu_ref[...], preferred_element_type=F32)
        act = (gate * _sigmoid(gate) * up).astype(BF16)
        o_ref[...] = jnp.dot(act, wd_ref[...], preferred_element_type=F32)

    @pl.when(pl.program_id(0) >= nu_ref[0])
    def _():
        o_ref[...] = jnp.zeros_like(o_ref)


def _experts(tile_expert, n_used, xs, layer, w_gate, w_up, w_down):
    n_rows = xs.shape[0]
    te = EXPERT_TILE

    def row_map(i, tex, nu):
        return (jnp.maximum(jnp.minimum(i, nu[0] - 1), 0), 0)

    def w_map(i, tex, nu):
        return (layer, tex[i], 0, 0)

    return pl.pallas_call(
        _expert_body,
        grid_spec=pltpu.PrefetchScalarGridSpec(
            num_scalar_prefetch=2,
            grid=(n_rows // te,),
            in_specs=[pl.BlockSpec((te, D_MODEL), row_map),
                      pl.BlockSpec((None, None, D_MODEL, D_FF_EXPERT), w_map),
                      pl.BlockSpec((None, None, D_MODEL, D_FF_EXPERT), w_map),
                      pl.BlockSpec((None, None, D_FF_EXPERT, D_MODEL), w_map)],
            out_specs=pl.BlockSpec((te, D_MODEL), lambda i, tex, nu: (i, 0))),
        out_shape=jax.ShapeDtypeStruct((n_rows, D_MODEL), F32),
        compiler_params=_params(("arbitrary",)),
        name="moe_experts",
    )(tile_expert, n_used, xs, w_gate, w_up, w_down)


def _combine_body(dest_ref, ys_ref, x_ref, w_ref, gf_ref, o_ref, a_buf, b_buf, sem, *, tm, t):
    base = pl.program_id(0) * tm

    def row_copy(d, buf, r):
        return pltpu.make_async_copy(ys_ref.at[pl.ds(d, 1)], buf.at[pl.ds(r, 1)], sem)

    def issue(r, carry):
        row_copy(dest_ref[base + r], a_buf, r).start()
        row_copy(dest_ref[t + base + r], b_buf, r).start()
        return carry

    lax.fori_loop(0, tm, issue, 0)
    pltpu.make_async_copy(ys_ref.at[pl.ds(0, tm)], a_buf, sem).wait()
    pltpu.make_async_copy(ys_ref.at[pl.ds(0, tm)], b_buf, sem).wait()
    w = w_ref[...]
    y = w[:, 0:1] * a_buf[...] + w[:, 1:2] * b_buf[...]
    o_ref[...] = x_ref[...] + gf_ref[...] * y


def _combine(dest, ys, x, w_cols, gate_f):
    t, d = x.shape
    tm = COMBINE_TILE
    return pl.pallas_call(
        functools.partial(_combine_body, tm=tm, t=t),
        grid_spec=pltpu.PrefetchScalarGridSpec(
            num_scalar_prefetch=1,
            grid=(t // tm,),
            in_specs=[pl.BlockSpec(memory_space=pl.ANY),
                      pl.BlockSpec((tm, d), lambda i, dest: (i, 0)),
                      pl.BlockSpec((tm, 8), lambda i, dest: (i, 0)),
                      pl.BlockSpec((1, d), lambda i, dest: (0, 0))],
            out_specs=pl.BlockSpec((tm, d), lambda i, dest: (i, 0)),
            scratch_shapes=[pltpu.VMEM((tm, d), F32), pltpu.VMEM((tm, d), F32),
                            pltpu.SemaphoreType.DMA(())]),
        out_shape=jax.ShapeDtypeStruct((t, d), F32),
        compiler_params=_params(("arbitrary",)),
        name="moe_combine",
    )(dest, ys, x, w_cols, gate_f)


def _pad_lanes(v, fill=0.0):
    return jnp.pad(v.astype(F32), (0, LANES - v.shape[0]), constant_values=fill).reshape(1, LANES)


def _routing_tables(route, counts, t):
    te = EXPERT_TILE
    n_tiles = (2 * t) // te + N_EXPERTS
    cnt = counts[:, 0].astype(jnp.int32)
    padded = ((cnt + te - 1) // te) * te
    ends = jnp.cumsum(padded)
    offsets = ends - padded
    e1 = route[0].astype(jnp.int32)
    e2 = route[1].astype(jnp.int32)
    d1 = offsets[e1] + route[2].astype(jnp.int32)
    d2 = offsets[e2] + route[3].astype(jnp.int32)
    dest = jnp.concatenate([d1, d2]).astype(jnp.int32)
    n_used = (ends[-1] // te).astype(jnp.int32)
    tile_start = jnp.arange(n_tiles, dtype=jnp.int32) * te
    tile_expert = jnp.sum((tile_start[:, None] >= ends[None, :]).astype(jnp.int32), axis=1)
    last_expert = tile_expert[jnp.maximum(n_used - 1, 0)]
    tile_expert = jnp.where(jnp.arange(n_tiles) < n_used, tile_expert, last_expert)
    tile_expert = jnp.minimum(tile_expert, N_EXPERTS - 1).astype(jnp.int32)
    w_cols = jnp.transpose(route)
    w_cols = jnp.concatenate([w_cols[:, 4:6], w_cols[:, 0:6]], axis=1)
    return dest, tile_expert, n_used.reshape(1), w_cols, n_tiles * te


def kernel(x, c, w_ada, b_ada, g_norm_mix, w_in, b_fgate, g_q_fox, g_k_fox, conv_w, conv_b,
           dt_bias, a_log, d_skip, g_ssm_norm, w_branch_sb, w_branch_fox, w_branch_ssm, w_out,
           g_norm_ffn, w_router, b_router, w_e_gate, w_e_up, w_e_down):
    bsz, t, d = x.shape
    assert bsz == 1 and d == D_MODEL
    n_layers = w_ada.shape[0]
    xt = x.reshape(t, d)

    mod = _ada(c, w_ada, b_ada)
    expand = (jnp.arange(SSM_INNER)[None, :] // SSM_HEAD_DIM
              == jnp.arange(LANES)[:, None]).astype(F32)
    w_router_t = jnp.transpose(w_router)
    b_router_col = b_router.reshape(N_EXPERTS, 1)

    o_ff = 3 * SB_WIDTH + 3 * FOX_WIDTH
    o_z = o_ff + FOX_HEADS
    o_xbc = o_z + SSM_INNER
    o_dt = o_xbc + SSM_CONV_DIM
    o_gate = o_dt + SSM_HEADS
    width = w_in.shape[2]
    assert o_ff % IN_TILE == 0 and o_gate + N_BRANCH * d == width
    qkv_tiles = o_ff // IN_TILE
    mid_end = -(-o_gate // IN_TILE) * IN_TILE
    mid_tiles = (mid_end - o_ff) // IN_TILE
    full_end = width // IN_TILE * IN_TILE
    head_start = o_gate // LANES * LANES
    tail_pad = -(-(width - full_end) // LANES) * LANES
    q_scale = jnp.concatenate([jnp.full((SB_WIDTH,), HEAD_DIM ** -0.5 * LOG2E, F32),
                               jnp.ones((o_ff - SB_WIDTH,), F32)]).reshape(1, o_ff)
    w_gate_b, w_up_b, w_down_b = _cast_bf16(w_e_gate), _cast_bf16(w_e_up), _cast_bf16(w_e_down)
    w_tail = jnp.pad(lax.slice_in_dim(w_in, full_end, width, axis=2),
                     ((0, 0), (0, 0), (0, tail_pad - (width - full_end))))

    for layer in range(n_layers):
        m = mod[layer].reshape(6, 1, d)
        shift_m, scale_m, gate_m, shift_f, scale_f, gate_f = (m[i] for i in range(6))

        h = _norm(xt, g_norm_mix[layer].reshape(1, d), scale_m, shift_m)
        qkv = _proj(h, w_in, layer, 0, qkv_tiles, BF16, col_scale=q_scale, name="in_proj_qkv")
        mid = _proj(h, w_in, layer, qkv_tiles, mid_tiles, F32, name="in_proj_mid")
        gates = _proj(h, w_in, layer, mid_end // IN_TILE, (full_end - mid_end) // IN_TILE, BF16,
                      act="sigmoid", name="in_proj_gates")
        gates_tail = _mm(h, w_tail[layer], BF16, tail_pad, act="sigmoid", name="in_proj_gates_tail")

        o_sb = _sb_attention(qkv)
        qn, kn, f_col = _fox_prep(qkv, mid, _pad_lanes(b_fgate[layer]),
                                  g_q_fox[layer].reshape(1, HEAD_DIM),
                                  g_k_fox[layer].reshape(1, HEAD_DIM))
        f_row = jnp.transpose(f_col[:, :8])
        o_fox = _fox_attention(qn, kn, qkv, f_col, f_row)
        o_ssm = _ssd(mid, (o_z - o_ff, o_xbc - o_ff, o_dt - o_ff), conv_w[layer],
                     conv_b[layer].reshape(1, SSM_CONV_DIM),
                     _pad_lanes(dt_bias[layer]), _pad_lanes(-jnp.exp(a_log[layer].astype(F32))),
                     jnp.repeat(d_skip[layer].astype(F32), SSM_HEAD_DIM).reshape(1, SSM_INNER),
                     g_ssm_norm[layer].reshape(1, SSM_INNER), expand)

        merged = _merge(mid, gates, gates_tail, mid_end - head_start, o_gate - head_start,
                        o_sb, o_fox, o_ssm, w_branch_sb[layer].astype(BF16),
                        w_branch_fox[layer].astype(BF16), w_branch_ssm[layer].astype(BF16))
        xt, h_moe, route, counts = _outproj_route(
            merged, w_out[layer].astype(BF16), xt, gate_m, g_norm_ffn[layer].reshape(1, d),
            scale_f, shift_f, w_router_t, b_router_col)

        dest, tile_expert, n_used, w_cols, n_rows = _routing_tables(route, counts, t)
        xs = _dispatch(dest, h_moe, n_rows)
        ys = _experts(tile_expert, n_used, xs, layer, w_gate_b, w_up_b, w_down_b)
        xt = _combine(dest, ys, xt, w_cols, gate_f)
    return xt.reshape(bsz, t, d)
```
